```python
import math
import jax, jax.numpy as jnp
from jax import lax
import numpy as np

D_MODEL = 1024
BATCH = 16
SEQ = 4096
DEPTH = 2
DEC_BATCH = 8
DEC_SEQ = 64
PAST_LEN = 2048

CHUNK = 64
QBLK = 128
N_MIXERS = 4
W_MIX = D_MODEL
W_GROUP = W_MIX // N_MIXERS
CONV_A = 31
A_GROUPS = 4
H_B = 4
DQK_B = W_GROUP // (2 * H_B)
DV_B = W_GROUP // H_B
ROT_DIM = DQK_B // 4
ROPE_THETA = 500000.0
H_C = 4
DK_C = W_GROUP // H_C
DV_C = W_GROUP // H_C
CONV_SHORT = 4
H_D = 4
P_D = W_GROUP // H_D
G_D = 2
N_D = 128
XBC_D = W_GROUP + 2 * G_D * N_D
E_GROUPS = 4
E_PER_GROUP = 4
N_EXPERTS = E_GROUPS * E_PER_GROUP
TOP_K_IN_GROUP = 2
F_EXPERT = D_MODEL // 4
EPS = 1e-6
SPLIT_SIZES = (2 * W_GROUP, 3 * W_GROUP, 3 * W_GROUP, W_GROUP, H_C, H_C, W_GROUP, XBC_D, H_D)
N_IN = sum(SPLIT_SIZES)
F32 = jnp.float32

kernel_name = 'hybrid_chunk_causal_head_groups_step'


def rmsnorm(x, g):
    xf = x.astype(F32)
    y = xf * lax.rsqrt(jnp.mean(xf * xf, axis=-1, keepdims=True) + EPS)
    return (y * g.astype(F32)).astype(x.dtype)


def l2norm(x):
    return x * lax.rsqrt(jnp.sum(x * x, axis=-1, keepdims=True) + EPS)


def causal_dwconv(x, buf, w, bias):
    xp = jnp.concatenate([buf.astype(x.dtype), x], axis=1)
    y = lax.conv_general_dilated(xp, w[:, None, :].astype(x.dtype), window_strides=(1,), padding='VALID',
                                 dimension_numbers=('NWC', 'WIO', 'NWC'), feature_group_count=x.shape[-1])
    if bias is not None:
        y = y + bias.astype(x.dtype)
    return y, xp[:, xp.shape[1] - (w.shape[0] - 1):]


def to_chunks(t, c):
    b, l, h = t.shape[:3]
    t = t.reshape((b, l // c, c, h) + t.shape[3:])
    return jnp.moveaxis(t, (1, 3), (0, 2))


def from_chunks(t):
    n, b, h, c = t.shape[:4]
    t = jnp.moveaxis(t, (0, 2), (1, 3))
    return t.reshape((b, n * c, h) + t.shape[4:])


def causal_decay(cum):
    c = cum.shape[-1]
    idx = jnp.arange(c)
    lower = idx[:, None] >= idx[None, :]
    return jnp.exp(jnp.where(lower, cum[..., :, None] - cum[..., None, :], -jnp.inf))


def rope_partial(x, pos):
    half = ROT_DIM // 2
    inv = jnp.exp(jnp.arange(half, dtype=F32) * (-2.0 / ROT_DIM) * math.log(ROPE_THETA))
    ang = pos.astype(F32)[:, None] * inv[None, :]
    cos = jnp.cos(ang)[None, :, None, None, :]
    sin = jnp.sin(ang)[None, :, None, None, :]
    xr = x[..., :ROT_DIM].astype(F32)
    x1, x2 = xr[..., :half], xr[..., half:]
    rot = jnp.concatenate([x1 * cos - x2 * sin, x2 * cos + x1 * sin], axis=-1)
    return jnp.concatenate([rot.astype(x.dtype), x[..., ROT_DIM:]], axis=-1)


def conformer_conv_mixer(u, buf, conv_w, conv_b, ln_g, ln_b):
    b, l, _ = u.shape
    a = u[..., :W_GROUP] * jax.nn.sigmoid(u[..., W_GROUP:])
    c, new_buf = causal_dwconv(a, buf, conv_w, conv_b)
    cf = c.astype(F32).reshape(b, l, A_GROUPS, W_GROUP // A_GROUPS)
    mu = jnp.mean(cf, axis=-1, keepdims=True)
    var = jnp.mean(jnp.square(cf - mu), axis=-1, keepdims=True)
    cn = ((cf - mu) * lax.rsqrt(var + EPS)).reshape(b, l, W_GROUP) * ln_g.astype(F32) + ln_b.astype(F32)
    return jax.nn.silu(cn).astype(u.dtype), new_buf


def diff_attention_block(q, k, v, q_pos, k_pos, lam):
    s = jnp.einsum('bqhcd,bkhcd->bhcqk', q, k, preferred_element_type=F32) * (DQK_B ** -0.5)
    mask = (k_pos[None, :] // CHUNK) <= (q_pos[:, None] // CHUNK)
    p = jax.nn.softmax(jnp.where(mask, s, -jnp.inf), axis=-1)
    pd = p[:, :, 0] - lam * p[:, :, 1]
    return jnp.einsum('bhqk,bkhe->bqhe', pd.astype(v.dtype), v)


def diff_attention_mixer(qkv, pos, k_past, v_past, pos_past, qn_g, kn_g, lam_p, subln_g, lam_init):
    b, l, _ = qkv.shape
    q, k, v = jnp.split(qkv, 3, axis=-1)
    q = rope_partial(rmsnorm(q.reshape(b, l, H_B, 2, DQK_B), qn_g), pos)
    k = rope_partial(rmsnorm(k.reshape(b, l, H_B, 2, DQK_B), kn_g), pos)
    v = v.reshape(b, l, H_B, DV_B)
    lp = lam_p.astype(F32)
    lam = jnp.exp(jnp.sum(lp[0] * lp[1])) - jnp.exp(jnp.sum(lp[2] * lp[3])) + lam_init
    k_all = jnp.concatenate([k_past.astype(k.dtype), k], axis=1)
    v_all = jnp.concatenate([v_past.astype(v.dtype), v], axis=1)
    k_pos = jnp.concatenate([pos_past, pos])
    qb = min(QBLK, l)
    nb = l // qb
    q_blocks = jnp.swapaxes(q.reshape(b, nb, qb, H_B, 2, DQK_B), 0, 1)
    p_blocks = pos.reshape(nb, qb)
    o = lax.map(lambda a: diff_attention_block(a[0], k_all, v_all, a[1], k_pos, lam), (q_blocks, p_blocks))
    o = jnp.swapaxes(o, 0, 1).reshape(b, l, H_B, DV_B)
    o = rmsnorm(o, subln_g) * (1.0 - lam_init)
    return o.reshape(b, l, W_GROUP), k, v


def chunked_gated_delta(q, k, v, g, beta, s0):
    c = min(CHUNK, q.shape[1])
    q, k, v, g, beta = (to_chunks(t, c) for t in (q, k, v, g, beta))
    cum = jnp.cumsum(g, axis=-1)
    decay = causal_decay(cum)
    idx = jnp.arange(c)
    strict = idx[:, None] > idx[None, :]
    kk = jnp.einsum('nbhid,nbhjd->nbhij', k, k)
    m = jnp.where(strict, beta[..., :, None] * kk * decay, 0.0) + jnp.eye(c, dtype=F32)
    u = lax.linalg.triangular_solve(m, beta[..., None] * v, left_side=True, lower=True, unit_diagonal=True)
    w = lax.linalg.triangular_solve(m, (beta * jnp.exp(cum))[..., None] * k, left_side=True, lower=True,
                                    unit_diagonal=True)
    qk = jnp.einsum('nbhid,nbhjd->nbhij', q, k) * decay
    q_dec = q * jnp.exp(cum)[..., None]
    k_dec = k * jnp.exp(cum[..., -1:] - cum)[..., None]
    tot = jnp.exp(cum[..., -1])

    def step(s, inp):
        u_n, w_n, qd_n, kd_n, qk_n, t_n = inp
        v_new = u_n - jnp.einsum('bhcd,bhde->bhce', w_n, s)
        o_n = jnp.einsum('bhcd,bhde->bhce', qd_n, s) + jnp.einsum('bhij,bhje->bhie', qk_n, v_new)
        s = s * t_n[..., None, None] + jnp.einsum('bhcd,bhce->bhde', kd_n, v_new)
        return s, o_n

    s_last, o = lax.scan(step, s0, (u, w, q_dec, k_dec, qk, tot))
    return from_chunks(o), s_last


def gated_deltanet_mixer(qkv, z, b_raw, a_raw, buf, s0, conv_w, a_log, dt_bias, onorm_g):
    b, l, _ = qkv.shape
    c, new_buf = causal_dwconv(qkv, buf, conv_w, None)
    c = jax.nn.silu(c.astype(F32))
    q, k, v = jnp.split(c, 3, axis=-1)
    q = l2norm(q.reshape(b, l, H_C, DK_C)) * (DK_C ** -0.5)
    k = l2norm(k.reshape(b, l, H_C, DK_C))
    v = v.reshape(b, l, H_C, DV_C)
    beta = jax.nn.sigmoid(b_raw.astype(F32))
    g = -jnp.exp(a_log.astype(F32)) * jax.nn.softplus(a_raw.astype(F32) + dt_bias.astype(F32))
    o, s = chunked_gated_delta(q, k, v, g, beta, s0.astype(F32))
    o = rmsnorm(o, onorm_g) * jax.nn.silu(z.astype(F32).reshape(b, l, H_C, DV_C))
    return o.reshape(b, l, W_GROUP).astype(qkv.dtype), new_buf, s.astype(qkv.dtype)


def chunked_ssd(x, dt, a, bm, cm, h0):
    c = min(CHUNK, x.shape[1])
    rep = H_D // G_D
    bh = to_chunks(jnp.repeat(bm, rep, axis=2), c)
    ch = to_chunks(jnp.repeat(cm, rep, axis=2), c)
    xdt = to_chunks(x * dt[..., None], c)
    acs = jnp.cumsum(to_chunks(dt * a, c), axis=-1)
    scores = jnp.einsum('nbhis,nbhjs->nbhij', ch, bh) * causal_decay(acs)
    y_diag = jnp.einsum('nbhij,nbhjp->nbhip', scores, xdt)
    s_chunk = jnp.einsum('nbhjs,nbhjp->nbhps', bh * jnp.exp(acs[..., -1:] - acs)[..., None], xdt)

    def step(state, inp):
        s_n, t_n = inp
        return state * t_n[..., None, None] + s_n, state

    h_last, h_in = lax.scan(step, h0, (s_chunk, jnp.exp(acs[..., -1])))
    y_off = jnp.einsum('nbhis,nbhps->nbhip', ch * jnp.exp(acs)[..., None], h_in)
    return from_chunks(y_diag + y_off), h_last


def ssd_mixer(z, xbc, dt_raw, buf, h0, conv_w, conv_b, a_log, dt_bias, d_skip, norm_g):
    b, l, _ = xbc.shape
    c, new_buf = causal_dwconv(xbc, buf, conv_w, conv_b)
    c = jax.nn.silu(c.astype(F32))
    xs = c[..., :W_GROUP].reshape(b, l, H_D, P_D)
    bm = c[..., W_GROUP:W_GROUP + G_D * N_D].reshape(b, l, G_D, N_D)
    cm = c[..., W_GROUP + G_D * N_D:].reshape(b, l, G_D, N_D)
    dt = jax.nn.softplus(dt_raw.astype(F32) + dt_bias.astype(F32))
    a = -jnp.exp(a_log.astype(F32))
    y, h = chunked_ssd(xs, dt, a, bm, cm, h0.astype(F32))
    y = y + d_skip.astype(F32)[:, None] * xs
    y = y.reshape(b, l, W_GROUP) * jax.nn.silu(z.astype(F32))
    y = rmsnorm(y.reshape(b, l, G_D, W_GROUP // G_D), norm_g.reshape(G_D, W_GROUP // G_D)).reshape(b, l, W_GROUP)
    return y.astype(xbc.dtype), new_buf, h.astype(xbc.dtype)


def hier_moe(x, w_group, b_group, w_expert, b_expert, w1, w3, w2):
    b, l, d = x.shape
    t = x.reshape(b * l, d)
    p_group = jax.nn.softmax((t @ w_group).astype(F32) + b_group.astype(F32), axis=-1)
    g_onehot = jax.nn.one_hot(jnp.argmax(p_group, axis=-1), E_GROUPS, dtype=F32)
    w_grp = jnp.sum(p_group * g_onehot, axis=-1)
    logits_e = ((t @ w_expert).astype(F32) + b_expert.astype(F32)).reshape(-1, E_GROUPS, E_PER_GROUP)
    p_in = jax.nn.softmax(jnp.einsum('tge,tg->te', logits_e, g_onehot), axis=-1)
    top_v, top_i = lax.top_k(p_in, TOP_K_IN_GROUP)
    top_v = top_v / jnp.sum(top_v, axis=-1, keepdims=True)
    gate_in = jnp.einsum('tk,tke->te', top_v, jax.nn.one_hot(top_i, E_PER_GROUP, dtype=F32))
    gate = (g_onehot[:, :, None] * gate_in[:, None, :] * w_grp[:, None, None]).reshape(-1, N_EXPERTS)
    gate = gate.astype(t.dtype)
    y = jnp.zeros_like(t)
    for e in range(N_EXPERTS):
        hid = jax.nn.silu(t @ w1[e]) * (t @ w3[e])
        y = y + gate[:, e:e + 1] * (hid @ w2[e])
    return y.reshape(b, l, d)


def trunk_layer(x, pos, pos_past, buf_a, k_past, v_past, buf_c, s_c, buf_d, h_d, lam_init,
                norm_mix_g, w_in, w_out, conf_conv_w, conf_conv_b, conf_ln_g, conf_ln_b,
                diff_qnorm_g, diff_knorm_g, diff_lambda, diff_subln_g,
                gdn_conv_w, gdn_A_log, gdn_dt_bias, gdn_onorm_g,
                ssd_conv_w, ssd_conv_b, ssd_A_log, ssd_dt_bias, ssd_D, ssd_norm_g,
                norm_ffn_g, moe_w_group, moe_b_group, moe_w_expert, moe_b_expert, moe_w1, moe_w3, moe_w2):
    u = rmsnorm(x, norm_mix_g) @ w_in
    cuts = np.cumsum(SPLIT_SIZES)[:-1].tolist()
    a_glu, b_qkv, c_qkv, c_z, c_beta, c_a, d_z, d_xbc, d_dt = jnp.split(u, cuts, axis=-1)
    y_a, buf_a = conformer_conv_mixer(a_glu, buf_a, conf_conv_w, conf_conv_b, conf_ln_g, conf_ln_b)
    y_b, k_new, v_new = diff_attention_mixer(b_qkv, pos, k_past, v_past, pos_past, diff_qnorm_g, diff_knorm_g,
                                             diff_lambda, diff_subln_g, lam_init)
    y_c, buf_c, s_c = gated_deltanet_mixer(c_qkv, c_z, c_beta, c_a, buf_c, s_c, gdn_conv_w, gdn_A_log,
                                           gdn_dt_bias, gdn_onorm_g)
    y_d, buf_d, h_d = ssd_mixer(d_z, d_xbc, d_dt, buf_d, h_d, ssd_conv_w, ssd_conv_b, ssd_A_log, ssd_dt_bias,
                                ssd_D, ssd_norm_g)
    h = x + jnp.concatenate([y_a, y_b, y_c, y_d], axis=-1) @ w_out
    y = h + hier_moe(rmsnorm(h, norm_ffn_g), moe_w_group, moe_b_group, moe_w_expert, moe_b_expert,
                     moe_w1, moe_w3, moe_w2)
    return y, buf_a, k_new, v_new, buf_c, s_c, buf_d, h_d


def setup_inputs(seed: int = 0) -> dict:
    key = jax.random.key(seed)
    keys = jax.random.split(key, 48)
    counter = [0]

    def nk():
        counter[0] += 1
        return keys[counter[0] - 1]

    def nrm(shape, s=1.0):
        return s * jax.random.normal(nk(), shape, F32)

    def gain(shape):
        return 1.0 + nrm(shape, 0.01)

    def a_log(shape):
        return jnp.log(jax.random.uniform(nk(), shape, F32, 1.0, 16.0))

    def dt_bias(shape):
        dtv = jnp.exp(jax.random.uniform(nk(), shape, F32, math.log(1e-3), math.log(1e-1)))
        return dtv + jnp.log(-jnp.expm1(-dtv))

    L = DEPTH
    return {
        'x_prompt': nrm((BATCH, SEQ, D_MODEL)),
        'x_sample': nrm((DEC_BATCH, DEC_SEQ, D_MODEL)),
        'cache_conv_conformer': nrm((L, DEC_BATCH, CONV_A - 1, W_GROUP), 0.5),
        'cache_k_diff': nrm((L, DEC_BATCH, PAST_LEN, H_B, 2, DQK_B)),
        'cache_v_diff': nrm((L, DEC_BATCH, PAST_LEN, H_B, DV_B)),
        'cache_conv_delta': nrm((L, DEC_BATCH, CONV_SHORT - 1, 3 * W_GROUP)),
        'state_delta': nrm((L, DEC_BATCH, H_C, DK_C, DV_C), 0.1),
        'cache_conv_ssd': nrm((L, DEC_BATCH, CONV_SHORT - 1, XBC_D)),
        'state_ssd': nrm((L, DEC_BATCH, H_D, P_D, N_D), 0.1),
        'norm_mix_g': gain((L, D_MODEL)),
        'w_in': nrm((L, D_MODEL, N_IN), D_MODEL ** -0.5),
        'w_out': nrm((L, W_MIX, D_MODEL), W_MIX ** -0.5),
        'conf_conv_w': nrm((L, CONV_A, W_GROUP), CONV_A ** -0.5),
        'conf_conv_b': nrm((L, W_GROUP), 0.01),
        'conf_ln_g': gain((L, W_GROUP)),
        'conf_ln_b': nrm((L, W_GROUP), 0.01),
        'diff_qnorm_g': gain((L, DQK_B)),
        'diff_knorm_g': gain((L, DQK_B)),
        'diff_lambda': nrm((L, 4, DQK_B), 0.1),
        'diff_subln_g': gain((L, DV_B)),
        'gdn_conv_w': nrm((L, CONV_SHORT, 3 * W_GROUP), CONV_SHORT ** -0.5),
        'gdn_A_log': a_log((L, H_C)),
        'gdn_dt_bias': dt_bias((L, H_C)),
        'gdn_onorm_g': gain((L, DV_C)),
        'ssd_conv_w': nrm((L, CONV_SHORT, XBC_D), CONV_SHORT ** -0.5),
        'ssd_conv_b': nrm((L, XBC_D), 0.01),
        'ssd_A_log': a_log((L, H_D)),
        'ssd_dt_bias': dt_bias((L, H_D)),
        'ssd_D': gain((L, H_D)),
        'ssd_norm_g': gain((L, W_GROUP)),
        'norm_ffn_g': gain((L, D_MODEL)),
        'moe_w_group': nrm((L, D_MODEL, E_GROUPS), D_MODEL ** -0.5),
        'moe_b_group': nrm((L, E_GROUPS), 0.01),
        'moe_w_expert': nrm((L, D_MODEL, N_EXPERTS), D_MODEL ** -0.5),
        'moe_b_expert': nrm((L, N_EXPERTS), 0.01),
        'moe_w1': nrm((L, N_EXPERTS, D_MODEL, F_EXPERT), D_MODEL ** -0.5),
        'moe_w3': nrm((L, N_EXPERTS, D_MODEL, F_EXPERT), D_MODEL ** -0.5),
        'moe_w2': nrm((L, N_EXPERTS, F_EXPERT, D_MODEL), F_EXPERT ** -0.5),
    }


def reference(x_prompt, x_sample, cache_conv_conformer, cache_k_diff, cache_v_diff, cache_conv_delta,
              state_delta, cache_conv_ssd, state_ssd, norm_mix_g, w_in, w_out, conf_conv_w, conf_conv_b,
              conf_ln_g, conf_ln_b, diff_qnorm_g, diff_knorm_g, diff_lambda, diff_subln_g, gdn_conv_w,
              gdn_A_log, gdn_dt_bias, gdn_onorm_g, ssd_conv_w, ssd_conv_b, ssd_A_log, ssd_dt_bias, ssd_D,
              ssd_norm_g, norm_ffn_g, moe_w_group, moe_b_group, moe_w_expert, moe_b_expert, moe_w1, moe_w3,
              moe_w2):
    bp, lp = x_prompt.shape[:2]
    ls = x_sample.shape[1]
    past_len = cache_k_diff.shape[2]
    dt = x_prompt.dtype
    pos_p = jnp.arange(lp, dtype=jnp.int32)
    pos_s = past_len + jnp.arange(ls, dtype=jnp.int32)
    pos_none = jnp.arange(0, dtype=jnp.int32)
    pos_past = jnp.arange(past_len, dtype=jnp.int32)
    y_prompt, y_sample = x_prompt, x_sample
    new_p = [[] for _ in range(7)]
    new_s = [[] for _ in range(7)]
    for l in range(DEPTH):
        lam_init = 0.8 - 0.6 * math.exp(-0.3 * l)
        wl = (norm_mix_g[l], w_in[l], w_out[l], conf_conv_w[l], conf_conv_b[l], conf_ln_g[l], conf_ln_b[l],
              diff_qnorm_g[l], diff_knorm_g[l], diff_lambda[l], diff_subln_g[l],
              gdn_conv_w[l], gdn_A_log[l], gdn_dt_bias[l], gdn_onorm_g[l],
              ssd_conv_w[l], ssd_conv_b[l], ssd_A_log[l], ssd_dt_bias[l], ssd_D[l], ssd_norm_g[l],
              norm_ffn_g[l], moe_w_group[l], moe_b_group[l], moe_w_expert[l], moe_b_expert[l],
              moe_w1[l], moe_w3[l], moe_w2[l])
        y_prompt, *st_p = trunk_layer(
            y_prompt, pos_p, pos_none,
            jnp.zeros((bp, CONV_A - 1, W_GROUP), dt),
            jnp.zeros((bp, 0, H_B, 2, DQK_B), dt), jnp.zeros((bp, 0, H_B, DV_B), dt),
            jnp.zeros((bp, CONV_SHORT - 1, 3 * W_GROUP), dt), jnp.zeros((bp, H_C, DK_C, DV_C), F32),
            jnp.zeros((bp, CONV_SHORT - 1, XBC_D), dt), jnp.zeros((bp, H_D, P_D, N_D), F32),
            lam_init, *wl)
        y_sample, *st_s = trunk_layer(
            y_sample, pos_s, pos_past,
            cache_conv_conformer[l], cache_k_diff[l], cache_v_diff[l],
            cache_conv_delta[l], state_delta[l], cache_conv_ssd[l], state_ssd[l],
            lam_init, *wl)
        for i in range(7):
            new_p[i].append(st_p[i])
            new_s[i].append(st_s[i])
    conv_conformer_p, k_diff_p, v_diff_p, conv_delta_p, state_delta_p, conv_ssd_p, state_ssd_p = [
        jnp.stack(s, axis=0) for s in new_p]
    conv_conformer_s, k_diff_s, v_diff_s, conv_delta_s, state_delta_s, conv_ssd_s, state_ssd_s = [
        jnp.stack(s, axis=0) for s in new_s]
    return (y_prompt, y_sample,
            conv_conformer_p, k_diff_p, v_diff_p, conv_delta_p, state_delta_p, conv_ssd_p, state_ssd_p,
            conv_conformer_s, k_diff_s, v_diff_s, conv_delta_s, state_delta_s, conv_ssd_s, state_ssd_s)
```

```python
import functools
import math

import numpy as np
import jax
import jax.numpy as jnp
from jax import lax
from jax.experimental import pallas as pl
from jax.experimental.pallas import tpu as pltpu

F32 = jnp.float32
BF16 = jnp.bfloat16

CHUNK = 64
N_MIXERS = 4
CONV_A = 31
A_GROUPS = 4
H_B = 4
ROPE_THETA = 500000.0
H_C = 4
CONV_SHORT = 4
H_D = 4
G_D = 2
N_D = 128
E_GROUPS = 4
E_PER_GROUP = 4
N_EXPERTS = E_GROUPS * E_PER_GROUP
EPS = 1e-6
NEG_BIG = -1e30

LANES = 128
CONV_A_PAD = 32
SHORT_PAD = 8
VMEM_LIMIT = 56 * 1024 * 1024


def _bdot(a, b):
    return jnp.dot(a.astype(BF16), b.astype(BF16), preferred_element_type=F32)


def _bdot_nt(a, b):
    return lax.dot_general(a.astype(BF16), b.astype(BF16), (((1,), (1,)), ((), ())),
                           preferred_element_type=F32)


def _dot3(a, b):
    a_hi = a.astype(BF16)
    a_lo = (a - a_hi.astype(F32)).astype(BF16)
    b_hi = b.astype(BF16)
    b_lo = (b - b_hi.astype(F32)).astype(BF16)
    return (jnp.dot(a_hi, b_hi, preferred_element_type=F32) + jnp.dot(a_hi, b_lo, preferred_element_type=F32)
            + jnp.dot(a_lo, b_hi, preferred_element_type=F32))


def _split_dot(x, m, n_split=3):
    acc = None
    r = x
    for i in range(n_split):
        p = r.astype(BF16)
        d = jnp.dot(p, m, preferred_element_type=F32)
        acc = d if acc is None else acc + d
        if i + 1 < n_split:
            r = r - p.astype(F32)
    return acc


def _split_dot_rhs(m, x, n_split=3):
    acc = None
    r = x
    for i in range(n_split):
        p = r.astype(BF16)
        d = jnp.dot(m, p, preferred_element_type=F32)
        acc = d if acc is None else acc + d
        if i + 1 < n_split:
            r = r - p.astype(F32)
    return acc


def _sigmoid(x):
    return 1.0 / (1.0 + jnp.exp(-x))


def _silu(x):
    return x * _sigmoid(x)


def _softplus(x):
    return jnp.maximum(x, 0.0) + jnp.log1p(jnp.exp(-jnp.abs(x)))


def _tile4(a):
    return jnp.concatenate([a, a, a, a], axis=0)


def _fold4(a, n):
    return a[0:n] + a[n:2 * n] + a[2 * n:3 * n] + a[3 * n:4 * n]


def _cparams(sem):
    return pltpu.CompilerParams(dimension_semantics=sem, vmem_limit_bytes=VMEM_LIMIT)


def _const_spec(shape):
    nd = len(shape)
    return pl.BlockSpec(shape, lambda *_: (0,) * nd)


def _in_proj_kernel(x_ref, g_ref, w_ref, *out_refs, segs):
    x = x_ref[...]
    ms = jnp.mean(x * x, axis=-1, keepdims=True)
    xn = (x * lax.rsqrt(ms + EPS) * g_ref[...]).astype(BF16)
    off = 0
    for o_ref, n in zip(out_refs, segs):
        o_ref[...] = jnp.dot(xn, w_ref[:, off:off + n], preferred_element_type=F32)
        off += n


def _in_proj(x2d, g, w_pad, segs):
    t, d = x2d.shape
    tm = min(512, t)
    n_all = sum(segs)
    return pl.pallas_call(
        functools.partial(_in_proj_kernel, segs=segs),
        out_shape=[jax.ShapeDtypeStruct((t, n), F32) for n in segs],
        grid=(t // tm,),
        in_specs=[pl.BlockSpec((tm, d), lambda i: (i, 0)),
                  _const_spec((1, d)),
                  _const_spec((d, n_all))],
        out_specs=[pl.BlockSpec((tm, n), lambda i: (i, 0)) for n in segs],
        compiler_params=_cparams(("parallel",)),
        name="in_proj",
    )(x2d, g, w_pad)


def _conf_kernel(u_ref, cache_ref, w_ref, b_ref, g_ref, beta_ref, pm_ref, y_ref, tail_ref, abuf, *, tl, wg):
    t = pl.program_id(1)

    @pl.when(t == 0)
    def _():
        abuf[0:CONV_A_PAD, :] = cache_ref[0]

    u = u_ref[0]
    a = u[:, :wg] * _sigmoid(u[:, wg:])
    abuf[CONV_A_PAD:CONV_A_PAD + tl, :] = a
    acc = jnp.zeros((tl, wg), F32) + b_ref[...]
    base = CONV_A_PAD - (CONV_A - 1)
    for j in range(CONV_A):
        acc = acc + w_ref[j:j + 1, :] * abuf[base + j:base + j + tl, :]
    pm = pm_ref[...]
    mu = _split_dot(acc, pm)
    dlt = acc - mu
    var = _split_dot(dlt * dlt, pm)
    cn = dlt * lax.rsqrt(var + EPS) * g_ref[...] + beta_ref[...]
    y_ref[0] = _silu(cn)
    tail = abuf[tl:tl + CONV_A_PAD, :]
    tail_ref[0] = tail
    abuf[0:CONV_A_PAD, :] = tail


def _conformer(u_a, cache_pad, w_pad, b, g, beta, pm):
    bsz, l, two_wg = u_a.shape
    wg = two_wg // 2
    tl = min(512, l)
    return pl.pallas_call(
        functools.partial(_conf_kernel, tl=tl, wg=wg),
        out_shape=[jax.ShapeDtypeStruct((bsz, l, wg), F32),
                   jax.ShapeDtypeStruct((bsz, CONV_A_PAD, wg), F32)],
        grid=(bsz, l // tl),
        in_specs=[pl.BlockSpec((1, tl, two_wg), lambda b_, t_: (b_, t_, 0)),
                  pl.BlockSpec((1, CONV_A_PAD, wg), lambda b_, t_: (b_, 0, 0)),
                  _const_spec((CONV_A_PAD, wg)),
                  _const_spec((1, wg)), _const_spec((1, wg)), _const_spec((1, wg)),
                  _const_spec((wg, wg))],
        out_specs=[pl.BlockSpec((1, tl, wg), lambda b_, t_: (b_, t_, 0)),
                   pl.BlockSpec((1, CONV_A_PAD, wg), lambda b_, t_: (b_, 0, 0))],
        scratch_shapes=[pltpu.VMEM((CONV_A_PAD + tl, wg), F32)],
        compiler_params=_cparams(("parallel", "arbitrary")),
        name="conformer",
    )(u_a, cache_pad, w_pad, b, g, beta, pm)


def _attn_prep_kernel(qkv_ref, cos_ref, sin_ref, qg_ref, kg_ref, pm_ref, eye_ref,
                      q_ref, kt_ref, vb_ref, kn_ref, vn_ref, *, wg, dqk):
    qkv = qkv_ref[0]
    cos = cos_ref[...]
    sin = sin_ref[...]
    pm = pm_ref[...]
    lane = lax.broadcasted_iota(jnp.int32, (1, wg), 1)
    low = (lane % dqk) < (dqk // 8)

    def norm_rope(x, g):
        ms = _split_dot(x * x, pm)
        xn = x * lax.rsqrt(ms + EPS) * g
        rot = jnp.where(low, pltpu.roll(xn, wg - dqk // 8, 1), pltpu.roll(xn, dqk // 8, 1))
        return xn * cos + rot * sin

    q = norm_rope(qkv[:, :wg], qg_ref[...])
    k = norm_rope(qkv[:, wg:2 * wg], kg_ref[...])
    v = qkv[:, 2 * wg:]
    q_ref[0] = q.astype(BF16)
    kn_ref[0] = k
    vn_ref[0] = v
    vb_ref[0] = v.astype(BF16)
    kt_ref[0] = _bdot_nt(eye_ref[...], k).astype(BF16)


def _attn_prep(qkv, cos_t, sin_t, qg, kg, pm, eye, dqk):
    bsz, l, w3 = qkv.shape
    wg = w3 // 3
    tl = min(512, l)
    kern = functools.partial(_attn_prep_kernel, wg=wg, dqk=dqk)
    return pl.pallas_call(
        kern,
        out_shape=[jax.ShapeDtypeStruct((bsz, l, wg), BF16),
                   jax.ShapeDtypeStruct((bsz, wg, l), BF16),
                   jax.ShapeDtypeStruct((bsz, l, wg), BF16),
                   jax.ShapeDtypeStruct((bsz, l, wg), F32),
                   jax.ShapeDtypeStruct((bsz, l, wg), F32)],
        grid=(l // tl, bsz),
        in_specs=[pl.BlockSpec((1, tl, w3), lambda t_, b_: (b_, t_, 0)),
                  pl.BlockSpec((tl, wg), lambda t_, b_: (t_, 0)),
                  pl.BlockSpec((tl, wg), lambda t_, b_: (t_, 0)),
                  _const_spec((1, wg)), _const_spec((1, wg)),
                  _const_spec((wg, wg)), _const_spec((wg, wg))],
        out_specs=[pl.BlockSpec((1, tl, wg), lambda t_, b_: (b_, t_, 0)),
                   pl.BlockSpec((1, wg, tl), lambda t_, b_: (b_, 0, t_)),
                   pl.BlockSpec((1, tl, wg), lambda t_, b_: (b_, t_, 0)),
                   pl.BlockSpec((1, tl, wg), lambda t_, b_: (b_, t_, 0)),
                   pl.BlockSpec((1, tl, wg), lambda t_, b_: (b_, t_, 0))],
        compiler_params=_cparams(("parallel", "parallel")),
        name="attn_prep",
    )(qkv, cos_t, sin_t, qg, kg, pm, eye)


def _kt_kernel(k_ref, eye_ref, kt_ref):
    kt_ref[0] = _bdot_nt(eye_ref[...], k_ref[0]).astype(BF16)


def _transpose_keys(k, eye):
    bsz, p, w = k.shape
    tp = min(512, p)
    return pl.pallas_call(
        _kt_kernel,
        out_shape=jax.ShapeDtypeStruct((bsz, w, p), BF16),
        grid=(bsz, p // tp),
        in_specs=[pl.BlockSpec((1, tp, w), lambda b_, t_: (b_, t_, 0)), _const_spec((w, w))],
        out_specs=pl.BlockSpec((1, w, tp), lambda b_, t_: (b_, 0, t_)),
        compiler_params=_cparams(("parallel", "parallel")),
        name="transpose_keys",
    )(k, eye)


def _flash_kernel(*refs, tq, tkp, n_past, nq, dqk, dv, lam_init):
    if n_past:
        q_ref, kt_ref, v_ref, ktp_ref, vp_ref, lam_ref, sg_ref, o_ref = refs
    else:
        q_ref, kt_ref, v_ref, lam_ref, sg_ref, o_ref = refs
    i = pl.program_id(1)
    lp = lam_ref[...]
    lam = (jnp.exp(jnp.sum(lp[0:1] * lp[1:2], axis=-1, keepdims=True))
           - jnp.exp(jnp.sum(lp[2:3] * lp[3:4], axis=-1, keepdims=True)) + lam_init)
    sg = sg_ref[...]
    row_c = lax.broadcasted_iota(jnp.int32, (tq, tq), 0) // CHUNK
    col_c = lax.broadcasted_iota(jnp.int32, (tq, tq), 1) // CHUNK
    diag_mask = col_c <= row_c
    hw = 2 * dqk
    c2 = (dqk ** -0.5) * math.log2(math.e)

    def update(qm, ktm, vh, carry, mask):
        m, l, acc = carry
        s = jnp.dot(qm, ktm, preferred_element_type=F32)
        if mask is not None:
            s = jnp.where(mask, s, NEG_BIG)
        m_new = jnp.maximum(m, jnp.max(s, axis=-1, keepdims=True))
        alpha = jnp.exp2((m - m_new) * c2)
        p = jnp.exp2((s - m_new) * c2)
        l = alpha * l + jnp.sum(p, axis=-1, keepdims=True)
        acc = alpha * acc + jnp.dot(p.astype(BF16), vh, preferred_element_type=F32)
        return m_new, l, acc

    outs = []
    for h in range(H_B):
        qh = q_ref[0, :, hw * h:hw * (h + 1)]
        q0 = qh[:, :dqk]
        q1 = qh[:, dqk:]
        init = (jnp.full((tq, 1), NEG_BIG, F32), jnp.zeros((tq, 1), F32), jnp.zeros((tq, dv), F32))
        c0, c1 = init, init

        def both(c0, c1, kt, vh, mask):
            c0 = update(q0, kt[:dqk], vh, c0, mask)
            c1 = update(q1, kt[dqk:], vh, c1, mask)
            return c0, c1

        for j in range(n_past):
            kt = ktp_ref[0, hw * h:hw * (h + 1), j * tkp:(j + 1) * tkp]
            vh = vp_ref[0, j * tkp:(j + 1) * tkp, dv * h:dv * (h + 1)]
            c0, c1 = both(c0, c1, kt, vh, None)

        if nq > 1:
            def body(j, carry):
                c0, c1 = carry
                off = pl.multiple_of(j * tq, tq)
                kt = kt_ref[0, hw * h:hw * (h + 1), pl.ds(off, tq)]
                vh = v_ref[0, pl.ds(off, tq), dv * h:dv * (h + 1)]
                return both(c0, c1, kt, vh, None)

            c0, c1 = lax.fori_loop(0, i, body, (c0, c1))
            off = pl.multiple_of(i * tq, tq)
            kt = kt_ref[0, hw * h:hw * (h + 1), pl.ds(off, tq)]
            vh = v_ref[0, pl.ds(off, tq), dv * h:dv * (h + 1)]
        else:
            kt = kt_ref[0, hw * h:hw * (h + 1), :]
            vh = v_ref[0, :, dv * h:dv * (h + 1)]
        c0, c1 = both(c0, c1, kt, vh, diag_mask)

        o = c0[2] * (1.0 / c0[1]) - lam * (c1[2] * (1.0 / c1[1]))
        ms = jnp.mean(o * o, axis=-1, keepdims=True)
        outs.append(o * lax.rsqrt(ms + EPS) * sg * (1.0 - lam_init))
    o_ref[0] = jnp.concatenate(outs, axis=1)


def _flash(q, kt, v, ktp, vp, lam_p, sg, lam_init, dqk, dv):
    bsz, l, wg = q.shape
    tq = min(256, l)
    nq = l // tq
    n_past = 0
    tkp = 0
    args = [q, kt, v]
    in_specs = [pl.BlockSpec((1, tq, wg), lambda b_, i_: (b_, i_, 0)),
                pl.BlockSpec((1, wg, l), lambda b_, i_: (b_, 0, 0)),
                pl.BlockSpec((1, l, wg), lambda b_, i_: (b_, 0, 0))]
    if ktp is not None:
        p = ktp.shape[2]
        tkp = min(512, p)
        n_past = p // tkp
        args += [ktp, vp]
        in_specs += [pl.BlockSpec((1, wg, p), lambda b_, i_: (b_, 0, 0)),
                     pl.BlockSpec((1, p, wg), lambda b_, i_: (b_, 0, 0))]
    args += [lam_p, sg]
    in_specs += [_const_spec(lam_p.shape), _const_spec(sg.shape)]
    kern = functools.partial(_flash_kernel, tq=tq, tkp=tkp, n_past=n_past, nq=nq, dqk=dqk, dv=dv,
                             lam_init=lam_init)
    return pl.pallas_call(
        kern,
        out_shape=jax.ShapeDtypeStruct((bsz, l, wg), F32),
        grid=(bsz, nq),
        in_specs=in_specs,
        out_specs=pl.BlockSpec((1, tq, wg), lambda b_, i_: (b_, i_, 0)),
        compiler_params=_cparams(("parallel", "parallel")),
        name="flash_diff_attn",
    )(*args)


def _short_conv(x_ref, hist_ref, w_ref, xbuf, t, tl):
    @pl.when(t == 0)
    def _():
        xbuf[0:SHORT_PAD, :] = hist_ref[0]

    xbuf[SHORT_PAD:SHORT_PAD + tl, :] = x_ref[0]
    base = SHORT_PAD - (CONV_SHORT - 1)
    acc = w_ref[0:1, :] * xbuf[base:base + tl, :]
    for j in range(1, CONV_SHORT):
        acc = acc + w_ref[j:j + 1, :] * xbuf[base + j:base + j + tl, :]
    xbuf[0:SHORT_PAD, :] = xbuf[tl:tl + SHORT_PAD, :]
    return acc


def _bd_masks(n):
    r = lax.broadcasted_iota(jnp.int32, (n, n), 0)
    c = lax.broadcasted_iota(jnp.int32, (n, n), 1)
    head = (r // CHUNK) == (c // CHUNK)
    incl = head & (c <= r)
    strict = head & (c < r)
    return head, incl, strict, r == c


def _expand_col(x4, lane0, n):
    r = lax.broadcasted_iota(jnp.int32, (n, LANES), 0) // CHUNK
    ln = lax.broadcasted_iota(jnp.int32, (n, LANES), 1)
    return jnp.sum(jnp.where(ln == r + lane0, _tile4(x4), 0.0), axis=-1, keepdims=True)


def _expand_row(x1, lane0, n):
    r = lax.broadcasted_iota(jnp.int32, (n, LANES), 0) // CHUNK
    ln = lax.broadcasted_iota(jnp.int32, (n, LANES), 1)
    return jnp.sum(jnp.where(ln == r + lane0, jnp.broadcast_to(x1, (n, LANES)), 0.0), axis=-1, keepdims=True)


def _decay_matrix(cum_r, incl, n):
    cm = jnp.broadcast_to(cum_r, (n, n))
    return jnp.where(incl, jnp.exp(jnp.minimum(cm - cm.T, 0.0)), 0.0)


def _gdn_kernel(qkv_ref, z_ref, sm_ref, hist_ref, s0_ref, cw_ref, alog_ref, dtb_ref, og_ref, ones_ref, tril_ref,
                y_ref, sout_ref, xbuf, s_scr, q_s, k_s, v_s, b_s, g_s, o_s, *, tl, wg):
    t = pl.program_id(1)
    n = wg

    @pl.when(t == 0)
    def _():
        s_scr[...] = s0_ref[0]

    c = _silu(_short_conv(qkv_ref, hist_ref, cw_ref, xbuf, t, tl))
    ones_bd = ones_ref[...]
    q = c[:, :wg]
    k = c[:, wg:2 * wg]
    dk = wg // H_C
    q_s[...] = q * lax.rsqrt(_split_dot(q * q, ones_bd) + EPS) * (dk ** -0.5)
    k_s[...] = k * lax.rsqrt(_split_dot(k * k, ones_bd) + EPS)
    v_s[...] = c[:, 2 * wg:]
    sm = sm_ref[0]
    b_s[...] = _sigmoid(sm)
    g_s[...] = -jnp.exp(alog_ref[...]) * _softplus(sm + dtb_ref[...])

    head, incl, strict, eye = _bd_masks(n)
    eye_f = jnp.where(eye, 1.0, 0.0)
    tril = tril_ref[...]

    def chunk(ci, carry):
        off = pl.multiple_of(ci * CHUNK, CHUNK)
        sl = pl.ds(off, CHUNK)
        qc, kc, vc = q_s[sl, :], k_s[sl, :], v_s[sl, :]
        cum = _split_dot_rhs(tril, g_s[sl, :])
        beta_r = _expand_col(b_s[sl, :], 0, n)
        cum_r = _expand_col(cum, H_C, n)
        tot_r = _expand_row(cum[CHUNK - 1:CHUNK, :], H_C, n)
        kx = jnp.where(head, _tile4(kc), 0.0)
        qx = jnp.where(head, _tile4(qc), 0.0)
        vx = jnp.where(head, _tile4(vc), 0.0)
        dm = _decay_matrix(cum_r, incl, n)
        kk = _bdot_nt(kx, kx)
        a = jnp.where(strict, beta_r * kk * dm, 0.0)
        x = eye_f - a
        p = a
        for _ in range(5):
            p = _bdot(p, p)
            x = x + _bdot(x, p)
        x = x + _bdot(x, eye_f - x - _dot3(a, x))
        ecum = jnp.exp(cum_r)
        u = _dot3(x, beta_r * vx)
        w = _dot3(x, (beta_r * ecum) * kx)
        s = s_scr[...]
        vn = u - _bdot(w, s)
        o = _bdot(qx * ecum, s) + _bdot(_bdot_nt(qx, kx) * dm, vn)
        kd = kx * jnp.exp(tot_r - cum_r)
        s_scr[...] = s * jnp.exp(tot_r) + _bdot(kd.T, vn)
        o_s[sl, :] = _fold4(o, CHUNK)
        return carry

    lax.fori_loop(0, tl // CHUNK, chunk, 0)
    o = o_s[...]
    ms = _split_dot(o * o, ones_bd) * (1.0 / dk)
    y_ref[0] = o * lax.rsqrt(ms + EPS) * og_ref[...] * _silu(z_ref[0])
    sout_ref[0] = s_scr[...]


def _gdn(qkv, z, small, hist, s0_bd, cw, alog, dtb, og, ones_bd, tril):
    bsz, l, w3 = qkv.shape
    wg = w3 // 3
    tl = min(256, l)
    kern = functools.partial(_gdn_kernel, tl=tl, wg=wg)
    bt = lambda b_, t_: (b_, t_, 0)
    b0 = lambda b_, t_: (b_, 0, 0)
    return pl.pallas_call(
        kern,
        out_shape=[jax.ShapeDtypeStruct((bsz, l, wg), F32),
                   jax.ShapeDtypeStruct((bsz, wg, wg), F32)],
        grid=(bsz, l // tl),
        in_specs=[pl.BlockSpec((1, tl, w3), bt),
                  pl.BlockSpec((1, tl, wg), bt),
                  pl.BlockSpec((1, tl, LANES), bt),
                  pl.BlockSpec((1, SHORT_PAD, w3), b0),
                  pl.BlockSpec((1, wg, wg), b0),
                  _const_spec((SHORT_PAD, w3)),
                  _const_spec((1, LANES)), _const_spec((1, LANES)), _const_spec((1, wg)),
                  _const_spec((wg, wg)), _const_spec((CHUNK, CHUNK))],
        out_specs=[pl.BlockSpec((1, tl, wg), bt),
                   pl.BlockSpec((1, wg, wg), b0)],
        scratch_shapes=[pltpu.VMEM((SHORT_PAD + tl, w3), F32),
                        pltpu.VMEM((wg, wg), F32),
                        pltpu.VMEM((tl, wg), F32), pltpu.VMEM((tl, wg), F32), pltpu.VMEM((tl, wg), F32),
                        pltpu.VMEM((tl, LANES), F32), pltpu.VMEM((tl, LANES), F32),
                        pltpu.VMEM((tl, wg), F32)],
        compiler_params=_cparams(("parallel", "arbitrary")),
        name="gated_deltanet",
    )(qkv, z, small, hist, s0_bd, cw, alog, dtb, og, ones_bd, tril)


def _ssd_kernel(xbc_ref, z_ref, sm_ref, hist_ref, h0_ref, cw_ref, cb_ref, alog_ref, dtb_ref, dsk_ref, ng_ref,
                tril_ref, y_ref, hout_ref, xbuf, h_scr, x_s, b_s, c_s, dt_s, da_s, y_s, *, tl, wg):
    t = pl.program_id(1)
    n = wg

    @pl.when(t == 0)
    def _():
        h_scr[...] = h0_ref[0]

    c = _silu(_short_conv(xbc_ref, hist_ref, cw_ref, xbuf, t, tl) + cb_ref[...])
    xs = c[:, :wg]
    x_s[...] = xs
    b_s[...] = c[:, wg:wg + G_D * N_D]
    c_s[...] = c[:, wg + G_D * N_D:]
    dt = _softplus(sm_ref[0] + dtb_ref[...])
    dt_s[...] = dt
    da_s[...] = -jnp.exp(alog_ref[...]) * dt

    head, incl, _, _ = _bd_masks(n)
    tril = tril_ref[...]
    lane0 = H_C + H_C

    def chunk(ci, carry):
        off = pl.multiple_of(ci * CHUNK, CHUNK)
        sl = pl.ds(off, CHUNK)
        xc, bc, cc = x_s[sl, :], b_s[sl, :], c_s[sl, :]
        acs = _split_dot_rhs(tril, da_s[sl, :])
        dt_r = _expand_col(dt_s[sl, :], lane0, n)
        acs_r = _expand_col(acs, lane0, n)
        tot_r = _expand_row(acs[CHUNK - 1:CHUNK, :], lane0, n)
        xdt = jnp.where(head, _tile4(xc), 0.0) * dt_r
        rep = H_D // G_D
        cn = jnp.concatenate([cc[:, N_D * (hh // rep):N_D * (hh // rep + 1)] for hh in range(H_D)], axis=0)
        bn = jnp.concatenate([bc[:, N_D * (hh // rep):N_D * (hh // rep + 1)] for hh in range(H_D)], axis=0)
        dm = _decay_matrix(acs_r, incl, n)
        scores = _bdot_nt(cn, bn) * dm
        hs = h_scr[...]
        y_bd = _bdot(scores, xdt) + jnp.where(head, _bdot_nt(cn * jnp.exp(acs_r), hs), 0.0)
        bdec = bn * jnp.exp(tot_r - acs_r)
        h_scr[...] = hs * jnp.exp(tot_r) + _bdot(xdt.T, bdec)
        y_s[sl, :] = _fold4(y_bd, CHUNK)
        return carry

    lax.fori_loop(0, tl // CHUNK, chunk, 0)
    y = (y_s[...] + dsk_ref[...] * xs) * _silu(z_ref[0])
    gw = wg // G_D
    parts = []
    for g in range(G_D):
        yg = y[:, gw * g:gw * (g + 1)]
        ms = jnp.mean(yg * yg, axis=-1, keepdims=True)
        parts.append(yg * lax.rsqrt(ms + EPS))
    y_ref[0] = jnp.concatenate(parts, axis=1) * ng_ref[...]
    hout_ref[0] = h_scr[...]


def _ssd(xbc, z, small, hist, h0, cw, cb, alog, dtb, dsk, ng, tril):
    bsz, l, wc = xbc.shape
    wg = z.shape[2]
    tl = min(256, l)
    kern = functools.partial(_ssd_kernel, tl=tl, wg=wg)
    bt = lambda b_, t_: (b_, t_, 0)
    b0 = lambda b_, t_: (b_, 0, 0)
    return pl.pallas_call(
        kern,
        out_shape=[jax.ShapeDtypeStruct((bsz, l, wg), F32),
                   jax.ShapeDtypeStruct((bsz, wg, N_D), F32)],
        grid=(bsz, l // tl),
        in_specs=[pl.BlockSpec((1, tl, wc), bt),
                  pl.BlockSpec((1, tl, wg), bt),
                  pl.BlockSpec((1, tl, LANES), bt),
                  pl.BlockSpec((1, SHORT_PAD, wc), b0),
                  pl.BlockSpec((1, wg, N_D), b0),
                  _const_spec((SHORT_PAD, wc)), _const_spec((1, wc)),
                  _const_spec((1, LANES)), _const_spec((1, LANES)), _const_spec((1, wg)), _const_spec((1, wg)),
                  _const_spec((CHUNK, CHUNK))],
        out_specs=[pl.BlockSpec((1, tl, wg), bt),
                   pl.BlockSpec((1, wg, N_D), b0)],
        scratch_shapes=[pltpu.VMEM((SHORT_PAD + tl, wc), F32),
                        pltpu.VMEM((wg, N_D), F32),
                        pltpu.VMEM((tl, wg), F32), pltpu.VMEM((tl, G_D * N_D), F32), pltpu.VMEM((tl, G_D * N_D), F32),
                        pltpu.VMEM((tl, LANES), F32), pltpu.VMEM((tl, LANES), F32),
                        pltpu.VMEM((tl, wg), F32)],
        compiler_params=_cparams(("parallel", "arbitrary")),
        name="ssd",
    )(xbc, z, small, hist, h0, cw, cb, alog, dtb, dsk, ng, tril)


def _out_proj_kernel(x_ref, ya_ref, yb_ref, yc_ref, yd_ref, w_ref, o_ref, *, wg):
    acc = x_ref[...]
    for i, y_ref in enumerate((ya_ref, yb_ref, yc_ref, yd_ref)):
        acc = acc + jnp.dot(y_ref[...].astype(BF16), w_ref[wg * i:wg * (i + 1), :], preferred_element_type=F32)
    o_ref[...] = acc


def _out_proj(x2d, ys, w_out):
    t, d = x2d.shape
    wg = ys[0].shape[1]
    tm = min(512, t)
    row = lambda i: (i, 0)
    return pl.pallas_call(
        functools.partial(_out_proj_kernel, wg=wg),
        out_shape=jax.ShapeDtypeStruct((t, d), F32),
        grid=(t // tm,),
        in_specs=[pl.BlockSpec((tm, d), row)] + [pl.BlockSpec((tm, wg), row)] * 4 + [_const_spec(w_out.shape)],
        out_specs=pl.BlockSpec((tm, d), row),
        compiler_params=_cparams(("parallel",)),
        name="out_proj",
    )(x2d, *ys, w_out)


def _route(logits):
    lane = lax.broadcasted_iota(jnp.int32, logits.shape, 1).astype(F32)
    big = float(LANES)
    lg = jnp.where(lane < E_GROUPS, logits, NEG_BIG)
    mg = jnp.max(lg, axis=-1, keepdims=True)
    sg = jnp.sum(jnp.exp(lg - mg), axis=-1, keepdims=True)
    gidx = jnp.min(jnp.where(lg == mg, lane, big), axis=-1, keepdims=True)
    w_grp = 1.0 / sg
    lo = E_GROUPS + E_PER_GROUP * gidx
    sel = (lane >= lo) & (lane < lo + E_PER_GROUP)
    le = jnp.where(sel, logits, NEG_BIG)
    me = jnp.max(le, axis=-1, keepdims=True)
    pe = jnp.where(sel, jnp.exp(le - me), 0.0)
    p_in = pe / jnp.sum(pe, axis=-1, keepdims=True)
    v1 = jnp.max(p_in, axis=-1, keepdims=True)
    i1 = jnp.min(jnp.where(sel & (p_in == v1), lane, big), axis=-1, keepdims=True)
    rest = sel & (lane != i1)
    p2 = jnp.where(rest, p_in, -1.0)
    v2 = jnp.max(p2, axis=-1, keepdims=True)
    i2 = jnp.min(jnp.where(rest & (p2 == v2), lane, big), axis=-1, keepdims=True)
    den = v1 + v2
    gate = jnp.where(lane == i1, v1 / den, 0.0) + jnp.where(lane == i2, v2 / den, 0.0)
    return gate * w_grp


def _moe_kernel(h_ref, g_ref, wr_ref, br_ref, w13_ref, w2_ref, o_ref, xn_s, gate_s, *, f):
    e = pl.program_id(1)

    @pl.when(e == 0)
    def _():
        h = h_ref[...]
        ms = jnp.mean(h * h, axis=-1, keepdims=True)
        xn = h * lax.rsqrt(ms + EPS) * g_ref[...]
        xn_s[...] = xn.astype(BF16)
        logits = jnp.dot(xn_s[...], wr_ref[...], preferred_element_type=F32)
        gate_s[...] = _route(logits + br_ref[...])
        o_ref[...] = h

    xn = xn_s[...]
    hid = jnp.dot(xn, w13_ref[0], preferred_element_type=F32)
    act = _silu(hid[:, :f]) * hid[:, f:]
    lane = lax.broadcasted_iota(jnp.int32, gate_s.shape, 1)
    ge = jnp.sum(jnp.where(lane == e + E_GROUPS, gate_s[...], 0.0), axis=-1, keepdims=True)
    o_ref[...] += jnp.dot((act * ge).astype(BF16), w2_ref[0], preferred_element_type=F32)


def _moe(h2d, g, wr, br, w13, w2):
    t, d = h2d.shape
    ne, _, f2 = w13.shape
    f = f2 // 2
    tm = min(1024, t)
    row = lambda i, e: (i, 0)
    return pl.pallas_call(
        functools.partial(_moe_kernel, f=f),
        out_shape=jax.ShapeDtypeStruct((t, d), F32),
        grid=(t // tm, ne),
        in_specs=[pl.BlockSpec((tm, d), row),
                  _const_spec((1, d)), _const_spec((d, LANES)), _const_spec((1, LANES)),
                  pl.BlockSpec((1, d, f2), lambda i, e: (e, 0, 0)),
                  pl.BlockSpec((1, f, d), lambda i, e: (e, 0, 0))],
        out_specs=pl.BlockSpec((tm, d), row),
        scratch_shapes=[pltpu.VMEM((tm, d), BF16), pltpu.VMEM((tm, LANES), F32)],
        compiler_params=_cparams(("parallel", "arbitrary")),
        name="hier_moe",
    )(h2d, g, wr, br, w13, w2)


def _block_diag_const(n, blk, val):
    r = np.arange(n)
    return jnp.asarray(np.where((r[:, None] // blk) == (r[None, :] // blk), val, 0.0), dtype=BF16)


def _rope_tables(pos, wg, dqk):
    rot = dqk // 4
    half = rot // 2
    inv = jnp.exp(jnp.arange(half, dtype=F32) * (-2.0 / rot) * math.log(ROPE_THETA))
    ang = pos.astype(F32)[:, None] * inv[None, :]
    cos, sin = jnp.cos(ang), jnp.sin(ang)
    l = pos.shape[0]
    ones = jnp.ones((l, dqk - rot), F32)
    zeros = jnp.zeros((l, dqk - rot), F32)
    cos_d = jnp.concatenate([cos, cos, ones], axis=1)
    sin_d = jnp.concatenate([-sin, sin, zeros], axis=1)
    reps = wg // dqk
    return jnp.tile(cos_d, (1, reps)), jnp.tile(sin_d, (1, reps))


def _pad_lanes(v, lane0):
    out = jnp.zeros((1, LANES), F32)
    return out.at[0, lane0:lane0 + v.shape[0]].set(v.astype(F32))


def _layer_params(l, p, wg, d):
    sizes = (2 * wg, 3 * wg, 3 * wg, wg, H_C, H_C, wg, wg + 2 * G_D * N_D, H_D)
    cuts = np.cumsum((0,) + sizes)
    w_in = p['w_in'][l]
    seg = lambda i: w_in[:, cuts[i]:cuts[i + 1]]
    n_small = H_C + H_C + H_D
    w_pad = jnp.concatenate([seg(0), seg(1), seg(2), seg(3), seg(6), seg(7), seg(4), seg(5), seg(8),
                             jnp.zeros((d, LANES - n_small), F32)], axis=1).astype(BF16)
    segs = (sizes[0], sizes[1], sizes[2], sizes[3], sizes[6], sizes[7], LANES)
    dqk = wg // (2 * H_B)
    row = lambda v: v.astype(F32).reshape(1, -1)
    wc = sizes[7]
    lp = dict(
        segs=segs, w_in=w_pad, norm_mix_g=row(p['norm_mix_g'][l]),
        w_out=p['w_out'][l].astype(BF16),
        conf_w=jnp.concatenate([p['conf_conv_w'][l], jnp.zeros((CONV_A_PAD - CONV_A, wg), F32)], axis=0),
        conf_b=row(p['conf_conv_b'][l]), conf_g=row(p['conf_ln_g'][l]), conf_beta=row(p['conf_ln_b'][l]),
        qg=row(jnp.tile(p['diff_qnorm_g'][l], wg // dqk)), kg=row(jnp.tile(p['diff_knorm_g'][l], wg // dqk)),
        lam_p=p['diff_lambda'][l].astype(F32), subln_g=row(p['diff_subln_g'][l]),
        gdn_w=jnp.concatenate([p['gdn_conv_w'][l], jnp.zeros((SHORT_PAD - CONV_SHORT, 3 * wg), F32)], axis=0),
        gdn_alog=_pad_lanes(p['gdn_A_log'][l], H_C), gdn_dtb=_pad_lanes(p['gdn_dt_bias'][l], H_C),
        gdn_og=row(jnp.tile(p['gdn_onorm_g'][l], H_C)),
        ssd_w=jnp.concatenate([p['ssd_conv_w'][l], jnp.zeros((SHORT_PAD - CONV_SHORT, wc), F32)], axis=0),
        ssd_b=row(p['ssd_conv_b'][l]),
        ssd_alog=_pad_lanes(p['ssd_A_log'][l], 2 * H_C), ssd_dtb=_pad_lanes(p['ssd_dt_bias'][l], 2 * H_C),
        ssd_dsk=row(jnp.repeat(p['ssd_D'][l], wg // H_D)), ssd_ng=row(p['ssd_norm_g'][l]),
        norm_ffn_g=row(p['norm_ffn_g'][l]),
        moe_wr=jnp.concatenate([p['moe_w_group'][l], p['moe_w_expert'][l],
                                jnp.zeros((d, LANES - E_GROUPS - N_EXPERTS), F32)], axis=1).astype(BF16),
        moe_br=jnp.concatenate([p['moe_b_group'][l], p['moe_b_expert'][l],
                                jnp.zeros((LANES - E_GROUPS - N_EXPERTS,), F32)]).reshape(1, LANES),
        moe_w13=jnp.concatenate([p['moe_w1'][l], p['moe_w3'][l]], axis=-1).astype(BF16),
        moe_w2=p['moe_w2'][l].astype(BF16),
    )
    return lp


def _trunk_layer(x, lp, consts, pos, lam_init, buf_a, k_past, v_past, buf_c, s_c, buf_d, h_d):
    bsz, l, d = x.shape
    wg = d // N_MIXERS
    dqk = wg // (2 * H_B)
    dv = wg // H_B
    assert wg // H_C == CHUNK and wg // H_D == CHUNK and H_C == H_D == N_MIXERS
    assert l % CHUNK == 0 and l >= CONV_A_PAD
    x2d = x.reshape(bsz * l, d)
    u_a, u_b, u_c, u_cz, u_dz, u_d, u_s = _in_proj(x2d, lp['norm_mix_g'], lp['w_in'], lp['segs'])
    r3 = lambda a: a.reshape(bsz, l, a.shape[-1])
    u_a, u_b, u_c, u_cz, u_dz, u_d, u_s = map(r3, (u_a, u_b, u_c, u_cz, u_dz, u_d, u_s))

    cache_pad = jnp.concatenate([jnp.zeros((bsz, CONV_A_PAD - (CONV_A - 1), wg), F32), buf_a.astype(F32)], axis=1)
    y_a, tail_a = _conformer(u_a, cache_pad, lp['conf_w'], lp['conf_b'], lp['conf_g'], lp['conf_beta'],
                             consts['mean_a'])
    new_buf_a = tail_a[:, CONV_A_PAD - (CONV_A - 1):, :]

    cos_t, sin_t = _rope_tables(pos, wg, dqk)
    q_b, kt_b, v_b, k_new, v_new = _attn_prep(u_b, cos_t, sin_t, lp['qg'], lp['kg'], consts['mean_qk'],
                                              consts['eye'], dqk)
    if k_past is not None:
        p = k_past.shape[1]
        ktp = _transpose_keys(k_past.reshape(bsz, p, wg).astype(F32), consts['eye'])
        vp = v_past.reshape(bsz, p, wg).astype(BF16)
    else:
        ktp, vp = None, None
    y_b = _flash(q_b, kt_b, v_b, ktp, vp, lp['lam_p'], lp['subln_g'], lam_init, dqk, dv)

    hist_c = jnp.concatenate([jnp.zeros((bsz, SHORT_PAD - (CONV_SHORT - 1), 3 * wg), F32), buf_c.astype(F32)], axis=1)
    s0_bd = jnp.tile(s_c.astype(F32).reshape(bsz, wg, wg // H_C), (1, 1, H_C)) * consts['head_mask']
    y_c, s_bd = _gdn(u_c, u_cz, u_s, hist_c, s0_bd, lp['gdn_w'], lp['gdn_alog'], lp['gdn_dtb'], lp['gdn_og'],
                     consts['ones_c'], consts['tril'])
    dk = wg // H_C
    s_new = jnp.stack([s_bd[:, dk * h:dk * (h + 1), dk * h:dk * (h + 1)] for h in range(H_C)], axis=1)
    new_buf_c = u_c[:, l - (CONV_SHORT - 1):, :]

    wc = u_d.shape[-1]
    hist_d = jnp.concatenate([jnp.zeros((bsz, SHORT_PAD - (CONV_SHORT - 1), wc), F32), buf_d.astype(F32)], axis=1)
    h0 = h_d.astype(F32).reshape(bsz, wg, N_D)
    y_d, h_new = _ssd(u_d, u_dz, u_s, hist_d, h0, lp['ssd_w'], lp['ssd_b'], lp['ssd_alog'], lp['ssd_dtb'],
                      lp['ssd_dsk'], lp['ssd_ng'], consts['tril'])
    h_new = h_new.reshape(bsz, H_D, wg // H_D, N_D)
    new_buf_d = u_d[:, l - (CONV_SHORT - 1):, :]

    r2 = lambda a: a.reshape(bsz * l, wg)
    h2d = _out_proj(x2d, [r2(y_a), r2(y_b), r2(y_c), r2(y_d)], lp['w_out'])
    y2d = _moe(h2d, lp['norm_ffn_g'], lp['moe_wr'], lp['moe_br'], lp['moe_w13'], lp['moe_w2'])
    y = y2d.reshape(bsz, l, d)
    k_new = k_new.reshape(bsz, l, H_B, 2, dqk)
    v_new = v_new.reshape(bsz, l, H_B, dv)
    return y, new_buf_a, k_new, v_new, new_buf_c, s_new, new_buf_d, h_new


@jax.jit
def _forward(x_prompt, x_sample, cache_conv_conformer, cache_k_diff, cache_v_diff, cache_conv_delta,
             state_delta, cache_conv_ssd, state_ssd, params):
    bp, lp_, d = x_prompt.shape
    ls = x_sample.shape[1]
    depth = params['w_in'].shape[0]
    past_len = cache_k_diff.shape[2]
    wg = d // N_MIXERS
    dqk = wg // (2 * H_B)
    consts = dict(
        mean_a=_block_diag_const(wg, wg // A_GROUPS, 1.0 / (wg // A_GROUPS)),
        mean_qk=_block_diag_const(wg, dqk, 1.0 / dqk),
        ones_c=_block_diag_const(wg, wg // H_C, 1.0),
        eye=jnp.eye(wg, dtype=BF16),
        tril=jnp.asarray(np.tril(np.ones((CHUNK, CHUNK))), dtype=BF16),
        head_mask=jnp.asarray(np.kron(np.eye(H_C), np.ones((wg // H_C, wg // H_C))), dtype=F32),
    )
    pos_p = jnp.arange(lp_, dtype=jnp.int32)
    pos_s = past_len + jnp.arange(ls, dtype=jnp.int32)
    y_p, y_s = x_prompt, x_sample
    new_p = [[] for _ in range(7)]
    new_s = [[] for _ in range(7)]
    wc = wg + 2 * G_D * N_D
    for l in range(depth):
        lam_init = 0.8 - 0.6 * math.exp(-0.3 * l)
        lp = _layer_params(l, params, wg, d)
        y_p, *st_p = _trunk_layer(
            y_p, lp, consts, pos_p, lam_init,
            jnp.zeros((bp, CONV_A - 1, wg), F32), None, None,
            jnp.zeros((bp, CONV_SHORT - 1, 3 * wg), F32), jnp.zeros((bp, H_C, wg // H_C, wg // H_C), F32),
            jnp.zeros((bp, CONV_SHORT - 1, wc), F32), jnp.zeros((bp, H_D, wg // H_D, N_D), F32))
        y_s, *st_s = _trunk_layer(
            y_s, lp, consts, pos_s, lam_init,
            cache_conv_conformer[l], cache_k_diff[l], cache_v_diff[l],
            cache_conv_delta[l], state_delta[l], cache_conv_ssd[l], state_ssd[l])
        for i in range(7):
            new_p[i].append(st_p[i])
            new_s[i].append(st_s[i])
    outs_p = [jnp.stack(s, axis=0) for s in new_p]
    outs_s = [jnp.stack(s, axis=0) for s in new_s]
    return (y_p, y_s, *outs_p, *outs_s)


def kernel(x_prompt, x_sample, cache_conv_conformer, cache_k_diff, cache_v_diff, cache_conv_delta, state_delta, cache_conv_ssd, state_ssd, norm_mix_g, w_in, w_out, conf_conv_w, conf_conv_b, conf_ln_g, conf_ln_b, diff_qnorm_g, diff_knorm_g, diff_lambda, diff_subln_g, gdn_conv_w, gdn_A_log, gdn_dt_bias, gdn_onorm_g, ssd_conv_w, ssd_conv_b, ssd_A_log, ssd_dt_bias, ssd_D, ssd_norm_g, norm_ffn_g, moe_w_group, moe_b_group, moe_w_expert, moe_b_expert, moe_w1, moe_w3, moe_w2):
    params = dict(
        norm_mix_g=norm_mix_g, w_in=w_in, w_out=w_out, conf_conv_w=conf_conv_w, conf_conv_b=conf_conv_b,
        conf_ln_g=conf_ln_g, conf_ln_b=conf_ln_b, diff_qnorm_g=diff_qnorm_g, diff_knorm_g=diff_knorm_g,
        diff_lambda=diff_lambda, diff_subln_g=diff_subln_g, gdn_conv_w=gdn_conv_w, gdn_A_log=gdn_A_log,
        gdn_dt_bias=gdn_dt_bias, gdn_onorm_g=gdn_onorm_g, ssd_conv_w=ssd_conv_w, ssd_conv_b=ssd_conv_b,
        ssd_A_log=ssd_A_log, ssd_dt_bias=ssd_dt_bias, ssd_D=ssd_D, ssd_norm_g=ssd_norm_g,
        norm_ffn_g=norm_ffn_g, moe_w_group=moe_w_group, moe_b_group=moe_b_group, moe_w_expert=moe_w_expert,
        moe_b_expert=moe_b_expert, moe_w1=moe_w1, moe_w3=moe_w3, moe_w2=moe_w2)
    return _forward(x_prompt, x_sample, cache_conv_conformer, cache_k_diff, cache_v_diff, cache_conv_delta,
                    state_delta, cache_conv_ssd, state_ssd, params)
```

```python
import functools
import math

import numpy as np
import jax
import jax.numpy as jnp
from jax import lax
from jax.experimental import pallas as pl
from jax.experimental.pallas import tpu as pltpu

F32 = jnp.float32
BF16 = jnp.bfloat16

CHUNK = 64
N_MIXERS = 4
CONV_A = 31
A_GROUPS = 4
H_B = 4
ROPE_THETA = 500000.0
H_C = 4
CONV_SHORT = 4
H_D = 4
G_D = 2
N_D = 128
E_GROUPS = 4
E_PER_GROUP = 4
N_EXPERTS = E_GROUPS * E_PER_GROUP
EPS = 1e-6
NEG_BIG = -1e30

LANES = 128
CONV_A_PAD = 32
SHORT_PAD = 8
VMEM_LIMIT = 56 * 1024 * 1024


def _bdot(a, b):
    return jnp.dot(a.astype(BF16), b.astype(BF16), preferred_element_type=F32)


def _bdot_nt(a, b):
    return lax.dot_general(a.astype(BF16), b.astype(BF16), (((1,), (1,)), ((), ())),
                           preferred_element_type=F32)


def _dot3(a, b):
    a_hi = a.astype(BF16)
    a_lo = (a - a_hi.astype(F32)).astype(BF16)
    b_hi = b.astype(BF16)
    b_lo = (b - b_hi.astype(F32)).astype(BF16)
    return (jnp.dot(a_hi, b_hi, preferred_element_type=F32) + jnp.dot(a_hi, b_lo, preferred_element_type=F32)
            + jnp.dot(a_lo, b_hi, preferred_element_type=F32))


def _split_dot(x, m, n_split=3):
    acc = None
    r = x
    for i in range(n_split):
        p = r.astype(BF16)
        d = jnp.dot(p, m, preferred_element_type=F32)
        acc = d if acc is None else acc + d
        if i + 1 < n_split:
            r = r - p.astype(F32)
    return acc


def _split_dot_rhs(m, x, n_split=3):
    acc = None
    r = x
    for i in range(n_split):
        p = r.astype(BF16)
        d = jnp.dot(m, p, preferred_element_type=F32)
        acc = d if acc is None else acc + d
        if i + 1 < n_split:
            r = r - p.astype(F32)
    return acc


def _sigmoid(x):
    return 1.0 / (1.0 + jnp.exp(-x))


def _silu(x):
    return x * _sigmoid(x)


def _softplus(x):
    return jnp.maximum(x, 0.0) + jnp.log1p(jnp.exp(-jnp.abs(x)))


def _tile4(a):
    return jnp.concatenate([a, a, a, a], axis=0)


def _fold4(a, n):
    return a[0:n] + a[n:2 * n] + a[2 * n:3 * n] + a[3 * n:4 * n]


def _cparams(sem):
    return pltpu.CompilerParams(dimension_semantics=sem, vmem_limit_bytes=VMEM_LIMIT)


def _const_spec(shape):
    nd = len(shape)
    return pl.BlockSpec(shape, lambda *_: (0,) * nd)


def _in_proj_kernel(x_ref, g_ref, w_ref, *out_refs, segs):
    x = x_ref[...]
    ms = jnp.mean(x * x, axis=-1, keepdims=True)
    xn = (x * lax.rsqrt(ms + EPS) * g_ref[...]).astype(BF16)
    off = 0
    for o_ref, n in zip(out_refs, segs):
        o_ref[...] = jnp.dot(xn, w_ref[:, off:off + n], preferred_element_type=F32)
        off += n


def _in_proj(x2d, g, w_pad, segs):
    t, d = x2d.shape
    tm = min(512, t)
    n_all = sum(segs)
    return pl.pallas_call(
        functools.partial(_in_proj_kernel, segs=segs),
        out_shape=[jax.ShapeDtypeStruct((t, n), F32) for n in segs],
        grid=(t // tm,),
        in_specs=[pl.BlockSpec((tm, d), lambda i: (i, 0)),
                  _const_spec((1, d)),
                  _const_spec((d, n_all))],
        out_specs=[pl.BlockSpec((tm, n), lambda i: (i, 0)) for n in segs],
        compiler_params=_cparams(("parallel",)),
        name="in_proj",
    )(x2d, g, w_pad)


def _conf_kernel(u_ref, cache_ref, w_ref, b_ref, g_ref, beta_ref, pm_ref, y_ref, tail_ref, abuf, *, tl, wg):
    t = pl.program_id(1)

    @pl.when(t == 0)
    def _():
        abuf[0:CONV_A_PAD, :] = cache_ref[0]

    u = u_ref[0]
    a = u[:, :wg] * _sigmoid(u[:, wg:])
    abuf[CONV_A_PAD:CONV_A_PAD + tl, :] = a
    acc = jnp.zeros((tl, wg), F32) + b_ref[...]
    base = CONV_A_PAD - (CONV_A - 1)
    for j in range(CONV_A):
        acc = acc + w_ref[j:j + 1, :] * abuf[base + j:base + j + tl, :]
    pm = pm_ref[...]
    mu = _split_dot(acc, pm)
    dlt = acc - mu
    var = _split_dot(dlt * dlt, pm)
    cn = dlt * lax.rsqrt(var + EPS) * g_ref[...] + beta_ref[...]
    y_ref[0] = _silu(cn)
    tail = abuf[tl:tl + CONV_A_PAD, :]
    tail_ref[0] = tail
    abuf[0:CONV_A_PAD, :] = tail


def _conformer(u_a, cache_pad, w_pad, b, g, beta, pm):
    bsz, l, two_wg = u_a.shape
    wg = two_wg // 2
    tl = min(512, l)
    return pl.pallas_call(
        functools.partial(_conf_kernel, tl=tl, wg=wg),
        out_shape=[jax.ShapeDtypeStruct((bsz, l, wg), F32),
                   jax.ShapeDtypeStruct((bsz, CONV_A_PAD, wg), F32)],
        grid=(bsz, l // tl),
        in_specs=[pl.BlockSpec((1, tl, two_wg), lambda b_, t_: (b_, t_, 0)),
                  pl.BlockSpec((1, CONV_A_PAD, wg), lambda b_, t_: (b_, 0, 0)),
                  _const_spec((CONV_A_PAD, wg)),
                  _const_spec((1, wg)), _const_spec((1, wg)), _const_spec((1, wg)),
                  _const_spec((wg, wg))],
        out_specs=[pl.BlockSpec((1, tl, wg), lambda b_, t_: (b_, t_, 0)),
                   pl.BlockSpec((1, CONV_A_PAD, wg), lambda b_, t_: (b_, 0, 0))],
        scratch_shapes=[pltpu.VMEM((CONV_A_PAD + tl, wg), F32)],
        compiler_params=_cparams(("parallel", "arbitrary")),
        name="conformer",
    )(u_a, cache_pad, w_pad, b, g, beta, pm)


def _attn_prep_kernel(qkv_ref, cos_ref, sin_ref, qg_ref, kg_ref, pm_ref, eye_ref,
                      q_ref, kt_ref, vb_ref, kn_ref, vn_ref, *, wg, dqk):
    qkv = qkv_ref[0]
    cos = cos_ref[...]
    sin = sin_ref[...]
    pm = pm_ref[...]
    lane = lax.broadcasted_iota(jnp.int32, (1, wg), 1)
    low = (lane % dqk) < (dqk // 8)

    def norm_rope(x, g):
        ms = _split_dot(x * x, pm)
        xn = x * lax.rsqrt(ms + EPS) * g
        rot = jnp.where(low, pltpu.roll(xn, wg - dqk // 8, 1), pltpu.roll(xn, dqk // 8, 1))
        return xn * cos + rot * sin

    q = norm_rope(qkv[:, :wg], qg_ref[...])
    k = norm_rope(qkv[:, wg:2 * wg], kg_ref[...])
    v = qkv[:, 2 * wg:]
    q_ref[0] = q.astype(BF16)
    kn_ref[0] = k
    vn_ref[0] = v
    dv = 2 * dqk
    pad = jnp.where(lax.broadcasted_iota(jnp.int32, (v.shape[0], dv), 1) == 0, 1.0, 0.0)
    vb_ref[0] = jnp.concatenate(
        [blk for h in range(H_B) for blk in (v[:, dv * h:dv * (h + 1)], pad)], axis=1).astype(BF16)
    kt_ref[0] = _bdot_nt(eye_ref[...], k).astype(BF16)


def _attn_prep(qkv, cos_t, sin_t, qg, kg, pm, eye, dqk):
    bsz, l, w3 = qkv.shape
    wg = w3 // 3
    tl = min(512, l)
    kern = functools.partial(_attn_prep_kernel, wg=wg, dqk=dqk)
    return pl.pallas_call(
        kern,
        out_shape=[jax.ShapeDtypeStruct((bsz, l, wg), BF16),
                   jax.ShapeDtypeStruct((bsz, wg, l), BF16),
                   jax.ShapeDtypeStruct((bsz, l, 2 * wg), BF16),
                   jax.ShapeDtypeStruct((bsz, l, wg), F32),
                   jax.ShapeDtypeStruct((bsz, l, wg), F32)],
        grid=(l // tl, bsz),
        in_specs=[pl.BlockSpec((1, tl, w3), lambda t_, b_: (b_, t_, 0)),
                  pl.BlockSpec((tl, wg), lambda t_, b_: (t_, 0)),
                  pl.BlockSpec((tl, wg), lambda t_, b_: (t_, 0)),
                  _const_spec((1, wg)), _const_spec((1, wg)),
                  _const_spec((wg, wg)), _const_spec((wg, wg))],
        out_specs=[pl.BlockSpec((1, tl, wg), lambda t_, b_: (b_, t_, 0)),
                   pl.BlockSpec((1, wg, tl), lambda t_, b_: (b_, 0, t_)),
                   pl.BlockSpec((1, tl, 2 * wg), lambda t_, b_: (b_, t_, 0)),
                   pl.BlockSpec((1, tl, wg), lambda t_, b_: (b_, t_, 0)),
                   pl.BlockSpec((1, tl, wg), lambda t_, b_: (b_, t_, 0))],
        compiler_params=_cparams(("parallel", "parallel")),
        name="attn_prep",
    )(qkv, cos_t, sin_t, qg, kg, pm, eye)


def _kt_kernel(k_ref, eye_ref, kt_ref):
    kt_ref[0] = _bdot_nt(eye_ref[...], k_ref[0]).astype(BF16)


def _transpose_keys(k, eye):
    bsz, p, w = k.shape
    tp = min(512, p)
    return pl.pallas_call(
        _kt_kernel,
        out_shape=jax.ShapeDtypeStruct((bsz, w, p), BF16),
        grid=(bsz, p // tp),
        in_specs=[pl.BlockSpec((1, tp, w), lambda b_, t_: (b_, t_, 0)), _const_spec((w, w))],
        out_specs=pl.BlockSpec((1, w, tp), lambda b_, t_: (b_, 0, t_)),
        compiler_params=_cparams(("parallel", "parallel")),
        name="transpose_keys",
    )(k, eye)


def _flash_kernel(*refs, tq, tkb, tkp, n_past, nq, dqk, dv, lam_init):
    if n_past:
        q_ref, kt_ref, v_ref, ktp_ref, vp_ref, lam_ref, sg_ref, o_ref = refs
    else:
        q_ref, kt_ref, v_ref, lam_ref, sg_ref, o_ref = refs
    i = pl.program_id(1)
    lp = lam_ref[...]
    lam = (jnp.exp(jnp.sum(lp[0:1] * lp[1:2], axis=-1, keepdims=True))
           - jnp.exp(jnp.sum(lp[2:3] * lp[3:4], axis=-1, keepdims=True)) + lam_init)
    sg = sg_ref[...]
    row_c = lax.broadcasted_iota(jnp.int32, (tq, tq), 0) // CHUNK
    col_c = lax.broadcasted_iota(jnp.int32, (tq, tq), 1) // CHUNK
    diag_mask = col_c <= row_c
    hw = 2 * dqk
    vw = 2 * dv
    c2 = (dqk ** -0.5) * math.log2(math.e)

    def update(qm, ktm, vt, carry, mask):
        m, acc = carry
        s = jnp.dot(qm, ktm, preferred_element_type=F32)
        if mask is not None:
            s = jnp.where(mask, s, NEG_BIG)
        m_new = jnp.maximum(m, jnp.max(s, axis=-1, keepdims=True))
        alpha = jnp.exp2((m - m_new) * c2)
        p = jnp.exp2((s - m_new) * c2)
        acc = alpha * acc + jnp.dot(p.astype(BF16), vt, preferred_element_type=F32)
        return m_new, acc

    outs = []
    for h in range(H_B):
        qh = q_ref[0, :, hw * h:hw * (h + 1)]
        q0 = qh[:, :dqk]
        q1 = qh[:, dqk:]
        rows = slice(hw * h, hw * (h + 1))
        cols = slice(vw * h, vw * (h + 1))
        init = (jnp.full((tq, 1), NEG_BIG, F32), jnp.zeros((tq, vw), F32))

        def both(carry, kt, vt, mask):
            c0, c1 = carry
            return update(q0, kt[:dqk], vt, c0, mask), update(q1, kt[dqk:], vt, c1, mask)

        if nq == 1:
            kts = [ktp_ref[0, rows, :]] if n_past else []
            vts = [vp_ref[0, :, cols]] if n_past else []
            kt = jnp.concatenate(kts + [kt_ref[0, rows, :]], axis=1)
            vt = jnp.concatenate(vts + [v_ref[0, :, cols]], axis=0)
            n_old = kt.shape[1] - tq
            vis = jnp.concatenate([jnp.full((tq, n_old), True), diag_mask], axis=1) if n_old else diag_mask

            def probs(qm, ktm):
                s = jnp.dot(qm, ktm, preferred_element_type=F32) * (dqk ** -0.5)
                s = jnp.where(vis, s, NEG_BIG)
                e = jnp.exp(s - jnp.max(s, axis=-1, keepdims=True))
                return e / jnp.sum(e, axis=-1, keepdims=True)

            pd = probs(q0, kt[:dqk]) - lam * probs(q1, kt[dqk:])
            o = jnp.dot(pd.astype(BF16), vt, preferred_element_type=F32)[:, :dv]
            ms = jnp.mean(o * o, axis=-1, keepdims=True)
            outs.append(o * lax.rsqrt(ms + EPS) * sg * (1.0 - lam_init))
            continue

        carry = (init, init)
        for j in range(n_past):
            carry = both(carry, ktp_ref[0, rows, j * tkp:(j + 1) * tkp], vp_ref[0, j * tkp:(j + 1) * tkp, cols], None)

        if nq > 1:
            start = i * tq
            n_big = start // tkb if tkb > tq else 0

            def body_big(j, carry):
                off = pl.multiple_of(j * tkb, tkb)
                return both(carry, kt_ref[0, rows, pl.ds(off, tkb)], v_ref[0, pl.ds(off, tkb), cols], None)

            def body_small(j, carry):
                off = pl.multiple_of(n_big * tkb + j * tq, tq)
                return both(carry, kt_ref[0, rows, pl.ds(off, tq)], v_ref[0, pl.ds(off, tq), cols], None)

            if tkb > tq:
                carry = lax.fori_loop(0, n_big, body_big, carry)
            carry = lax.fori_loop(0, (start - n_big * tkb) // tq, body_small, carry)
            off = pl.multiple_of(start, tq)
            carry = both(carry, kt_ref[0, rows, pl.ds(off, tq)], v_ref[0, pl.ds(off, tq), cols], diag_mask)

        (_, a0), (_, a1) = carry
        o = a0[:, :dv] * (1.0 / a0[:, dv:dv + 1]) - lam * (a1[:, :dv] * (1.0 / a1[:, dv:dv + 1]))
        ms = jnp.mean(o * o, axis=-1, keepdims=True)
        outs.append(o * lax.rsqrt(ms + EPS) * sg * (1.0 - lam_init))
    o_ref[0] = jnp.concatenate(outs, axis=1)


def _flash(q, kt, v1, ktp, vp1, lam_p, sg, lam_init, dqk, dv):
    bsz, l, wg = q.shape
    vw_all = v1.shape[2]
    tq = min(512, l)
    tkb = min(1024, l)
    nq = l // tq
    n_past = 0
    tkp = 0
    args = [q, kt, v1]
    in_specs = [pl.BlockSpec((1, tq, wg), lambda b_, i_: (b_, i_, 0)),
                pl.BlockSpec((1, wg, l), lambda b_, i_: (b_, 0, 0)),
                pl.BlockSpec((1, l, vw_all), lambda b_, i_: (b_, 0, 0))]
    if ktp is not None:
        p = ktp.shape[2]
        tkp = min(2048, p)
        n_past = p // tkp
        args += [ktp, vp1]
        in_specs += [pl.BlockSpec((1, wg, p), lambda b_, i_: (b_, 0, 0)),
                     pl.BlockSpec((1, p, vw_all), lambda b_, i_: (b_, 0, 0))]
    args += [lam_p, sg]
    in_specs += [_const_spec(lam_p.shape), _const_spec(sg.shape)]
    kern = functools.partial(_flash_kernel, tq=tq, tkb=tkb, tkp=tkp, n_past=n_past, nq=nq, dqk=dqk, dv=dv,
                             lam_init=lam_init)
    return pl.pallas_call(
        kern,
        out_shape=jax.ShapeDtypeStruct((bsz, l, wg), F32),
        grid=(bsz, nq),
        in_specs=in_specs,
        out_specs=pl.BlockSpec((1, tq, wg), lambda b_, i_: (b_, i_, 0)),
        compiler_params=_cparams(("parallel", "parallel")),
        name="flash_diff_attn",
    )(*args)


def _short_conv(x_ref, hist_ref, w_ref, xbuf, t, tl):
    @pl.when(t == 0)
    def _():
        xbuf[0:SHORT_PAD, :] = hist_ref[0]

    xbuf[SHORT_PAD:SHORT_PAD + tl, :] = x_ref[0]
    base = SHORT_PAD - (CONV_SHORT - 1)
    acc = w_ref[0:1, :] * xbuf[base:base + tl, :]
    for j in range(1, CONV_SHORT):
        acc = acc + w_ref[j:j + 1, :] * xbuf[base + j:base + j + tl, :]
    xbuf[0:SHORT_PAD, :] = xbuf[tl:tl + SHORT_PAD, :]
    return acc


def _bd_masks(n):
    r = lax.broadcasted_iota(jnp.int32, (n, n), 0)
    c = lax.broadcasted_iota(jnp.int32, (n, n), 1)
    head = (r // CHUNK) == (c // CHUNK)
    incl = head & (c <= r)
    strict = head & (c < r)
    return head, incl, strict, r == c


def _expand_col(x4, lane0, n):
    r = lax.broadcasted_iota(jnp.int32, (n, LANES), 0) // CHUNK
    ln = lax.broadcasted_iota(jnp.int32, (n, LANES), 1)
    return jnp.sum(jnp.where(ln == r + lane0, _tile4(x4), 0.0), axis=-1, keepdims=True)


def _expand_row(x1, lane0, n):
    r = lax.broadcasted_iota(jnp.int32, (n, LANES), 0) // CHUNK
    ln = lax.broadcasted_iota(jnp.int32, (n, LANES), 1)
    return jnp.sum(jnp.where(ln == r + lane0, jnp.broadcast_to(x1, (n, LANES)), 0.0), axis=-1, keepdims=True)


def _decay_matrix(cum_r, incl, n):
    cm = jnp.broadcast_to(cum_r, (n, n))
    return jnp.where(incl, jnp.exp(jnp.minimum(cm - cm.T, 0.0)), 0.0)


def _gdn_kernel(qkv_ref, z_ref, sm_ref, hist_ref, s0_ref, cw_ref, alog_ref, dtb_ref, og_ref, ones_ref, tril_ref,
                y_ref, sout_ref, xbuf, s_scr, q_s, k_s, v_s, b_s, g_s, o_s, *, tl, wg):
    t = pl.program_id(1)
    n = wg

    @pl.when(t == 0)
    def _():
        s_scr[...] = s0_ref[0]

    c = _silu(_short_conv(qkv_ref, hist_ref, cw_ref, xbuf, t, tl))
    ones_bd = ones_ref[...]
    q = c[:, :wg]
    k = c[:, wg:2 * wg]
    dk = wg // H_C
    q_s[...] = q * lax.rsqrt(_split_dot(q * q, ones_bd) + EPS) * (dk ** -0.5)
    k_s[...] = k * lax.rsqrt(_split_dot(k * k, ones_bd) + EPS)
    v_s[...] = c[:, 2 * wg:]
    sm = sm_ref[0]
    b_s[...] = _sigmoid(sm)
    g_s[...] = -jnp.exp(alog_ref[...]) * _softplus(sm + dtb_ref[...])

    head, incl, strict, eye = _bd_masks(n)
    eye_f = jnp.where(eye, 1.0, 0.0)
    tril = tril_ref[...]

    sls = [slice(ci * CHUNK, (ci + 1) * CHUNK) for ci in range(tl // CHUNK)]
    cum = [_split_dot_rhs(tril, g_s[sl, :]) for sl in sls]
    beta_r = [_expand_col(b_s[sl, :], 0, n) for sl in sls]
    cum_r = [_expand_col(cm, H_C, n) for cm in cum]
    tot_r = [_expand_row(cm[CHUNK - 1:CHUNK, :], H_C, n) for cm in cum]
    kx = [jnp.where(head, _tile4(k_s[sl, :]), 0.0) for sl in sls]
    qx = [jnp.where(head, _tile4(q_s[sl, :]), 0.0) for sl in sls]
    vx = [jnp.where(head, _tile4(v_s[sl, :]), 0.0) for sl in sls]
    dm = [_decay_matrix(cr, incl, n) for cr in cum_r]
    a = [jnp.where(strict, br * _bdot_nt(kc, kc) * dc, 0.0) for br, kc, dc in zip(beta_r, kx, dm)]
    x = [eye_f - ac for ac in a]
    p = a
    for _ in range(int(math.log2(CHUNK)) - 2):
        p = [_bdot(pc, pc) for pc in p]
        x = [xc + _bdot(xc, pc) for xc, pc in zip(x, p)]
    x = [xc + _bdot(xc, eye_f - xc - _dot3(ac, xc)) for xc, ac in zip(x, a)]
    ecum = [jnp.exp(cr) for cr in cum_r]
    u = [_dot3(xc, br * vc) for xc, br, vc in zip(x, beta_r, vx)]
    w = [_dot3(xc, (br * ec) * kc) for xc, br, ec, kc in zip(x, beta_r, ecum, kx)]
    qe = [qc * ec for qc, ec in zip(qx, ecum)]
    qkd = [_bdot_nt(qc, kc) * dc for qc, kc, dc in zip(qx, kx, dm)]
    kdt = [(kc * jnp.exp(tr - cr)).T for kc, tr, cr in zip(kx, tot_r, cum_r)]
    s = s_scr[...]
    for ci, sl in enumerate(sls):
        vn = u[ci] - _bdot(w[ci], s)
        o = _bdot(qe[ci], s) + _bdot(qkd[ci], vn)
        o_s[sl, :] = _fold4(o, CHUNK)
        s = s * jnp.exp(tot_r[ci]) + _bdot(kdt[ci], vn)
    s_scr[...] = s
    o = o_s[...]
    ms = _split_dot(o * o, ones_bd) * (1.0 / dk)
    y_ref[0] = o * lax.rsqrt(ms + EPS) * og_ref[...] * _silu(z_ref[0])
    sout_ref[0] = s_scr[...]


def _gdn(qkv, z, small, hist, s0_bd, cw, alog, dtb, og, ones_bd, tril):
    bsz, l, w3 = qkv.shape
    wg = w3 // 3
    tl = min(256, l)
    kern = functools.partial(_gdn_kernel, tl=tl, wg=wg)
    bt = lambda b_, t_: (b_, t_, 0)
    b0 = lambda b_, t_: (b_, 0, 0)
    return pl.pallas_call(
        kern,
        out_shape=[jax.ShapeDtypeStruct((bsz, l, wg), F32),
                   jax.ShapeDtypeStruct((bsz, wg, wg), F32)],
        grid=(bsz, l // tl),
        in_specs=[pl.BlockSpec((1, tl, w3), bt),
                  pl.BlockSpec((1, tl, wg), bt),
                  pl.BlockSpec((1, tl, LANES), bt),
                  pl.BlockSpec((1, SHORT_PAD, w3), b0),
                  pl.BlockSpec((1, wg, wg), b0),
                  _const_spec((SHORT_PAD, w3)),
                  _const_spec((1, LANES)), _const_spec((1, LANES)), _const_spec((1, wg)),
                  _const_spec((wg, wg)), _const_spec((CHUNK, CHUNK))],
        out_specs=[pl.BlockSpec((1, tl, wg), bt),
                   pl.BlockSpec((1, wg, wg), b0)],
        scratch_shapes=[pltpu.VMEM((SHORT_PAD + tl, w3), F32),
                        pltpu.VMEM((wg, wg), F32),
                        pltpu.VMEM((tl, wg), F32), pltpu.VMEM((tl, wg), F32), pltpu.VMEM((tl, wg), F32),
                        pltpu.VMEM((tl, LANES), F32), pltpu.VMEM((tl, LANES), F32),
                        pltpu.VMEM((tl, wg), F32)],
        compiler_params=_cparams(("parallel", "arbitrary")),
        name="gated_deltanet",
    )(qkv, z, small, hist, s0_bd, cw, alog, dtb, og, ones_bd, tril)


def _ssd_kernel(xbc_ref, z_ref, sm_ref, hist_ref, h0_ref, cw_ref, cb_ref, alog_ref, dtb_ref, dsk_ref, ng_ref,
                tril_ref, y_ref, hout_ref, xbuf, h_scr, x_s, b_s, c_s, dt_s, da_s, y_s, *, tl, wg):
    t = pl.program_id(1)
    n = wg

    @pl.when(t == 0)
    def _():
        h_scr[...] = h0_ref[0]

    c = _silu(_short_conv(xbc_ref, hist_ref, cw_ref, xbuf, t, tl) + cb_ref[...])
    xs = c[:, :wg]
    x_s[...] = xs
    b_s[...] = c[:, wg:wg + G_D * N_D]
    c_s[...] = c[:, wg + G_D * N_D:]
    dt = _softplus(sm_ref[0] + dtb_ref[...])
    dt_s[...] = dt
    da_s[...] = -jnp.exp(alog_ref[...]) * dt

    head, incl, _, _ = _bd_masks(n)
    tril = tril_ref[...]
    lane0 = H_C + H_C

    rep = H_D // G_D

    def per_head(a):
        return jnp.concatenate([a[:, N_D * (hh // rep):N_D * (hh // rep + 1)] for hh in range(H_D)], axis=0)

    sls = [slice(ci * CHUNK, (ci + 1) * CHUNK) for ci in range(tl // CHUNK)]
    acs = [_split_dot_rhs(tril, da_s[sl, :]) for sl in sls]
    dt_r = [_expand_col(dt_s[sl, :], lane0, n) for sl in sls]
    acs_r = [_expand_col(ac, lane0, n) for ac in acs]
    tot_r = [_expand_row(ac[CHUNK - 1:CHUNK, :], lane0, n) for ac in acs]
    xdt = [jnp.where(head, _tile4(x_s[sl, :]), 0.0) * dr for sl, dr in zip(sls, dt_r)]
    cn = [per_head(c_s[sl, :]) for sl in sls]
    bn = [per_head(b_s[sl, :]) for sl in sls]
    scores = [_bdot_nt(cc, bc) * _decay_matrix(ar, incl, n) for cc, bc, ar in zip(cn, bn, acs_r)]
    y_diag = [_bdot(sc, xc) for sc, xc in zip(scores, xdt)]
    s_chunk = [_bdot(xc.T, bc * jnp.exp(tr - ar)) for xc, bc, tr, ar in zip(xdt, bn, tot_r, acs_r)]
    c_dec = [cc * jnp.exp(ar) for cc, ar in zip(cn, acs_r)]
    hs = h_scr[...]
    for ci, sl in enumerate(sls):
        y_bd = y_diag[ci] + jnp.where(head, _bdot_nt(c_dec[ci], hs), 0.0)
        y_s[sl, :] = _fold4(y_bd, CHUNK)
        hs = hs * jnp.exp(tot_r[ci]) + s_chunk[ci]
    h_scr[...] = hs
    y = (y_s[...] + dsk_ref[...] * xs) * _silu(z_ref[0])
    gw = wg // G_D
    parts = []
    for g in range(G_D):
        yg = y[:, gw * g:gw * (g + 1)]
        ms = jnp.mean(yg * yg, axis=-1, keepdims=True)
        parts.append(yg * lax.rsqrt(ms + EPS))
    y_ref[0] = jnp.concatenate(parts, axis=1) * ng_ref[...]
    hout_ref[0] = h_scr[...]


def _ssd(xbc, z, small, hist, h0, cw, cb, alog, dtb, dsk, ng, tril):
    bsz, l, wc = xbc.shape
    wg = z.shape[2]
    tl = min(256, l)
    kern = functools.partial(_ssd_kernel, tl=tl, wg=wg)
    bt = lambda b_, t_: (b_, t_, 0)
    b0 = lambda b_, t_: (b_, 0, 0)
    return pl.pallas_call(
        kern,
        out_shape=[jax.ShapeDtypeStruct((bsz, l, wg), F32),
                   jax.ShapeDtypeStruct((bsz, wg, N_D), F32)],
        grid=(bsz, l // tl),
        in_specs=[pl.BlockSpec((1, tl, wc), bt),
                  pl.BlockSpec((1, tl, wg), bt),
                  pl.BlockSpec((1, tl, LANES), bt),
                  pl.BlockSpec((1, SHORT_PAD, wc), b0),
                  pl.BlockSpec((1, wg, N_D), b0),
                  _const_spec((SHORT_PAD, wc)), _const_spec((1, wc)),
                  _const_spec((1, LANES)), _const_spec((1, LANES)), _const_spec((1, wg)), _const_spec((1, wg)),
                  _const_spec((CHUNK, CHUNK))],
        out_specs=[pl.BlockSpec((1, tl, wg), bt),
                   pl.BlockSpec((1, wg, N_D), b0)],
        scratch_shapes=[pltpu.VMEM((SHORT_PAD + tl, wc), F32),
                        pltpu.VMEM((wg, N_D), F32),
                        pltpu.VMEM((tl, wg), F32), pltpu.VMEM((tl, G_D * N_D), F32), pltpu.VMEM((tl, G_D * N_D), F32),
                        pltpu.VMEM((tl, LANES), F32), pltpu.VMEM((tl, LANES), F32),
                        pltpu.VMEM((tl, wg), F32)],
        compiler_params=_cparams(("parallel", "arbitrary")),
        name="ssd",
    )(xbc, z, small, hist, h0, cw, cb, alog, dtb, dsk, ng, tril)


def _out_proj_kernel(x_ref, ya_ref, yb_ref, yc_ref, yd_ref, w_ref, o_ref, *, wg):
    acc = x_ref[...]
    for i, y_ref in enumerate((ya_ref, yb_ref, yc_ref, yd_ref)):
        acc = acc + jnp.dot(y_ref[...].astype(BF16), w_ref[wg * i:wg * (i + 1), :], preferred_element_type=F32)
    o_ref[...] = acc


def _out_proj(x2d, ys, w_out):
    t, d = x2d.shape
    wg = ys[0].shape[1]
    tm = min(512, t)
    row = lambda i: (i, 0)
    return pl.pallas_call(
        functools.partial(_out_proj_kernel, wg=wg),
        out_shape=jax.ShapeDtypeStruct((t, d), F32),
        grid=(t // tm,),
        in_specs=[pl.BlockSpec((tm, d), row)] + [pl.BlockSpec((tm, wg), row)] * 4 + [_const_spec(w_out.shape)],
        out_specs=pl.BlockSpec((tm, d), row),
        compiler_params=_cparams(("parallel",)),
        name="out_proj",
    )(x2d, *ys, w_out)


def _route(logits):
    lane = lax.broadcasted_iota(jnp.int32, logits.shape, 1).astype(F32)
    big = float(LANES)
    lg = jnp.where(lane < E_GROUPS, logits, NEG_BIG)
    mg = jnp.max(lg, axis=-1, keepdims=True)
    sg = jnp.sum(jnp.exp(lg - mg), axis=-1, keepdims=True)
    gidx = jnp.min(jnp.where(lg == mg, lane, big), axis=-1, keepdims=True)
    w_grp = 1.0 / sg
    lo = E_GROUPS + E_PER_GROUP * gidx
    sel = (lane >= lo) & (lane < lo + E_PER_GROUP)
    le = jnp.where(sel, logits, NEG_BIG)
    me = jnp.max(le, axis=-1, keepdims=True)
    pe = jnp.where(sel, jnp.exp(le - me), 0.0)
    p_in = pe / jnp.sum(pe, axis=-1, keepdims=True)
    v1 = jnp.max(p_in, axis=-1, keepdims=True)
    i1 = jnp.min(jnp.where(sel & (p_in == v1), lane, big), axis=-1, keepdims=True)
    rest = sel & (lane != i1)
    p2 = jnp.where(rest, p_in, -1.0)
    v2 = jnp.max(p2, axis=-1, keepdims=True)
    i2 = jnp.min(jnp.where(rest & (p2 == v2), lane, big), axis=-1, keepdims=True)
    den = v1 + v2
    gate = jnp.where(lane == i1, v1 / den, 0.0) + jnp.where(lane == i2, v2 / den, 0.0)
    return gate * w_grp


def _moe_kernel(h_ref, g_ref, wr_ref, br_ref, w13_ref, w2_ref, o_ref, act_s, *, f, ne):
    h = h_ref[...]
    ms = jnp.mean(h * h, axis=-1, keepdims=True)
    xn = (h * lax.rsqrt(ms + EPS) * g_ref[...]).astype(BF16)
    gate = _route(jnp.dot(xn, wr_ref[...], preferred_element_type=F32) + br_ref[...])
    for e in range(ne):
        hid = jnp.dot(xn, w13_ref[e], preferred_element_type=F32)
        act = _silu(hid[:, :f]) * hid[:, f:]
        act_s[:, f * e:f * (e + 1)] = (act * gate[:, E_GROUPS + e:E_GROUPS + e + 1]).astype(BF16)
    o_ref[...] = h + jnp.dot(act_s[...], w2_ref[...], preferred_element_type=F32)


def _moe(h2d, g, wr, br, w13, w2s):
    t, d = h2d.shape
    ne, _, f2 = w13.shape
    f = f2 // 2
    tm = min(512, t)
    row = lambda i: (i, 0)
    resident = dict(pipeline_mode=pl.Buffered(1))
    return pl.pallas_call(
        functools.partial(_moe_kernel, f=f, ne=ne),
        out_shape=jax.ShapeDtypeStruct((t, d), F32),
        grid=(t // tm,),
        in_specs=[pl.BlockSpec((tm, d), row),
                  _const_spec((1, d)), _const_spec((d, LANES)), _const_spec((1, LANES)),
                  pl.BlockSpec((ne, d, f2), lambda i: (0, 0, 0), **resident),
                  pl.BlockSpec((ne * f, d), lambda i: (0, 0), **resident)],
        out_specs=pl.BlockSpec((tm, d), row),
        scratch_shapes=[pltpu.VMEM((tm, ne * f), BF16)],
        compiler_params=_cparams(("parallel",)),
        name="hier_moe",
    )(h2d, g, wr, br, w13, w2s)


def _block_diag_const(n, blk, val):
    r = np.arange(n)
    return jnp.asarray(np.where((r[:, None] // blk) == (r[None, :] // blk), val, 0.0), dtype=BF16)


def _rope_tables(pos, wg, dqk):
    rot = dqk // 4
    half = rot // 2
    inv = jnp.exp(jnp.arange(half, dtype=F32) * (-2.0 / rot) * math.log(ROPE_THETA))
    ang = pos.astype(F32)[:, None] * inv[None, :]
    cos, sin = jnp.cos(ang), jnp.sin(ang)
    l = pos.shape[0]
    ones = jnp.ones((l, dqk - rot), F32)
    zeros = jnp.zeros((l, dqk - rot), F32)
    cos_d = jnp.concatenate([cos, cos, ones], axis=1)
    sin_d = jnp.concatenate([-sin, sin, zeros], axis=1)
    reps = wg // dqk
    return jnp.tile(cos_d, (1, reps)), jnp.tile(sin_d, (1, reps))


def _pad_lanes(v, lane0):
    out = jnp.zeros((1, LANES), F32)
    return out.at[0, lane0:lane0 + v.shape[0]].set(v.astype(F32))


def _layer_params(l, p, wg, d):
    sizes = (2 * wg, 3 * wg, 3 * wg, wg, H_C, H_C, wg, wg + 2 * G_D * N_D, H_D)
    cuts = np.cumsum((0,) + sizes)
    w_in = p['w_in'][l]
    seg = lambda i: w_in[:, cuts[i]:cuts[i + 1]]
    n_small = H_C + H_C + H_D
    w_pad = jnp.concatenate([seg(0), seg(1), seg(2), seg(3), seg(6), seg(7), seg(4), seg(5), seg(8),
                             jnp.zeros((d, LANES - n_small), F32)], axis=1).astype(BF16)
    segs = (sizes[0], sizes[1], sizes[2], sizes[3], sizes[6], sizes[7], LANES)
    dqk = wg // (2 * H_B)
    row = lambda v: v.astype(F32).reshape(1, -1)
    wc = sizes[7]
    lp = dict(
        segs=segs, w_in=w_pad, norm_mix_g=row(p['norm_mix_g'][l]),
        w_out=p['w_out'][l].astype(BF16),
        conf_w=jnp.concatenate([p['conf_conv_w'][l], jnp.zeros((CONV_A_PAD - CONV_A, wg), F32)], axis=0),
        conf_b=row(p['conf_conv_b'][l]), conf_g=row(p['conf_ln_g'][l]), conf_beta=row(p['conf_ln_b'][l]),
        qg=row(jnp.tile(p['diff_qnorm_g'][l], wg // dqk)), kg=row(jnp.tile(p['diff_knorm_g'][l], wg // dqk)),
        lam_p=p['diff_lambda'][l].astype(F32), subln_g=row(p['diff_subln_g'][l]),
        gdn_w=jnp.concatenate([p['gdn_conv_w'][l], jnp.zeros((SHORT_PAD - CONV_SHORT, 3 * wg), F32)], axis=0),
        gdn_alog=_pad_lanes(p['gdn_A_log'][l], H_C), gdn_dtb=_pad_lanes(p['gdn_dt_bias'][l], H_C),
        gdn_og=row(jnp.tile(p['gdn_onorm_g'][l], H_C)),
        ssd_w=jnp.concatenate([p['ssd_conv_w'][l], jnp.zeros((SHORT_PAD - CONV_SHORT, wc), F32)], axis=0),
        ssd_b=row(p['ssd_conv_b'][l]),
        ssd_alog=_pad_lanes(p['ssd_A_log'][l], 2 * H_C), ssd_dtb=_pad_lanes(p['ssd_dt_bias'][l], 2 * H_C),
        ssd_dsk=row(jnp.repeat(p['ssd_D'][l], wg // H_D)), ssd_ng=row(p['ssd_norm_g'][l]),
        norm_ffn_g=row(p['norm_ffn_g'][l]),
        moe_wr=jnp.concatenate([p['moe_w_group'][l], p['moe_w_expert'][l],
                                jnp.zeros((d, LANES - E_GROUPS - N_EXPERTS), F32)], axis=1).astype(BF16),
        moe_br=jnp.concatenate([p['moe_b_group'][l], p['moe_b_expert'][l],
                                jnp.zeros((LANES - E_GROUPS - N_EXPERTS,), F32)]).reshape(1, LANES),
        moe_w13=jnp.concatenate([p['moe_w1'][l], p['moe_w3'][l]], axis=-1).astype(BF16),
        moe_w2=p['moe_w2'][l].astype(BF16).reshape(-1, d),
    )
    return lp


def _trunk_layer(x, lp, consts, pos, lam_init, buf_a, k_past, v_past, buf_c, s_c, buf_d, h_d):
    bsz, l, d = x.shape
    wg = d // N_MIXERS
    dqk = wg // (2 * H_B)
    dv = wg // H_B
    assert wg // H_C == CHUNK and wg // H_D == CHUNK and H_C == H_D == N_MIXERS
    assert l % CHUNK == 0 and l >= CONV_A_PAD
    x2d = x.reshape(bsz * l, d)
    u_a, u_b, u_c, u_cz, u_dz, u_d, u_s = _in_proj(x2d, lp['norm_mix_g'], lp['w_in'], lp['segs'])
    r3 = lambda a: a.reshape(bsz, l, a.shape[-1])
    u_a, u_b, u_c, u_cz, u_dz, u_d, u_s = map(r3, (u_a, u_b, u_c, u_cz, u_dz, u_d, u_s))

    cache_pad = jnp.concatenate([jnp.zeros((bsz, CONV_A_PAD - (CONV_A - 1), wg), F32), buf_a.astype(F32)], axis=1)
    y_a, tail_a = _conformer(u_a, cache_pad, lp['conf_w'], lp['conf_b'], lp['conf_g'], lp['conf_beta'],
                             consts['mean_a'])
    new_buf_a = tail_a[:, CONV_A_PAD - (CONV_A - 1):, :]

    cos_t, sin_t = _rope_tables(pos, wg, dqk)
    q_b, kt_b, v_b, k_new, v_new = _attn_prep(u_b, cos_t, sin_t, lp['qg'], lp['kg'], consts['mean_qk'],
                                              consts['eye'], dqk)
    if k_past is not None:
        p = k_past.shape[1]
        ktp = _transpose_keys(k_past.reshape(bsz, p, wg).astype(F32), consts['eye'])
        vp4 = v_past.reshape(bsz, p, H_B, dv).astype(BF16)
        one = jnp.zeros((bsz, p, H_B, dv), BF16).at[..., 0].set(1.0)
        vp = jnp.concatenate([vp4, one], axis=-1).reshape(bsz, p, 2 * wg)
    else:
        ktp, vp = None, None
    y_b = _flash(q_b, kt_b, v_b, ktp, vp, lp['lam_p'], lp['subln_g'], lam_init, dqk, dv)

    hist_c = jnp.concatenate([jnp.zeros((bsz, SHORT_PAD - (CONV_SHORT - 1), 3 * wg), F32), buf_c.astype(F32)], axis=1)
    s0_bd = jnp.tile(s_c.astype(F32).reshape(bsz, wg, wg // H_C), (1, 1, H_C)) * consts['head_mask']
    y_c, s_bd = _gdn(u_c, u_cz, u_s, hist_c, s0_bd, lp['gdn_w'], lp['gdn_alog'], lp['gdn_dtb'], lp['gdn_og'],
                     consts['ones_c'], consts['tril'])
    dk = wg // H_C
    s_new = jnp.stack([s_bd[:, dk * h:dk * (h + 1), dk * h:dk * (h + 1)] for h in range(H_C)], axis=1)
    new_buf_c = u_c[:, l - (CONV_SHORT - 1):, :]

    wc = u_d.shape[-1]
    hist_d = jnp.concatenate([jnp.zeros((bsz, SHORT_PAD - (CONV_SHORT - 1), wc), F32), buf_d.astype(F32)], axis=1)
    h0 = h_d.astype(F32).reshape(bsz, wg, N_D)
    y_d, h_new = _ssd(u_d, u_dz, u_s, hist_d, h0, lp['ssd_w'], lp['ssd_b'], lp['ssd_alog'], lp['ssd_dtb'],
                      lp['ssd_dsk'], lp['ssd_ng'], consts['tril'])
    h_new = h_new.reshape(bsz, H_D, wg // H_D, N_D)
    new_buf_d = u_d[:, l - (CONV_SHORT - 1):, :]

    r2 = lambda a: a.reshape(bsz * l, wg)
    h2d = _out_proj(x2d, [r2(y_a), r2(y_b), r2(y_c), r2(y_d)], lp['w_out'])
    y2d = _moe(h2d, lp['norm_ffn_g'], lp['moe_wr'], lp['moe_br'], lp['moe_w13'], lp['moe_w2'])
    y = y2d.reshape(bsz, l, d)
    k_new = k_new.reshape(bsz, l, H_B, 2, dqk)
    v_new = v_new.reshape(bsz, l, H_B, dv)
    return y, new_buf_a, k_new, v_new, new_buf_c, s_new, new_buf_d, h_new


@jax.jit
def _forward(x_prompt, x_sample, cache_conv_conformer, cache_k_diff, cache_v_diff, cache_conv_delta,
             state_delta, cache_conv_ssd, state_ssd, params):
    bp, lp_, d = x_prompt.shape
    ls = x_sample.shape[1]
    depth = params['w_in'].shape[0]
    past_len = cache_k_diff.shape[2]
    wg = d // N_MIXERS
    dqk = wg // (2 * H_B)
    consts = dict(
        mean_a=_block_diag_const(wg, wg // A_GROUPS, 1.0 / (wg // A_GROUPS)),
        mean_qk=_block_diag_const(wg, dqk, 1.0 / dqk),
        ones_c=_block_diag_const(wg, wg // H_C, 1.0),
        eye=jnp.eye(wg, dtype=BF16),
        tril=jnp.asarray(np.tril(np.ones((CHUNK, CHUNK))), dtype=BF16),
        head_mask=jnp.asarray(np.kron(np.eye(H_C), np.ones((wg // H_C, wg // H_C))), dtype=F32),
    )
    pos_p = jnp.arange(lp_, dtype=jnp.int32)
    pos_s = past_len + jnp.arange(ls, dtype=jnp.int32)
    y_p, y_s = x_prompt, x_sample
    new_p = [[] for _ in range(7)]
    new_s = [[] for _ in range(7)]
    wc = wg + 2 * G_D * N_D
    for l in range(depth):
        lam_init = 0.8 - 0.6 * math.exp(-0.3 * l)
        lp = _layer_params(l, params, wg, d)
        y_p, *st_p = _trunk_layer(
            y_p, lp, consts, pos_p, lam_init,
            jnp.zeros((bp, CONV_A - 1, wg), F32), None, None,
            jnp.zeros((bp, CONV_SHORT - 1, 3 * wg), F32), jnp.zeros((bp, H_C, wg // H_C, wg // H_C), F32),
            jnp.zeros((bp, CONV_SHORT - 1, wc), F32), jnp.zeros((bp, H_D, wg // H_D, N_D), F32))
        y_s, *st_s = _trunk_layer(
            y_s, lp, consts, pos_s, lam_init,
            cache_conv_conformer[l], cache_k_diff[l], cache_v_diff[l],
            cache_conv_delta[l], state_delta[l], cache_conv_ssd[l], state_ssd[l])
        for i in range(7):
            new_p[i].append(st_p[i])
            new_s[i].append(st_s[i])
    outs_p = [jnp.stack(s, axis=0) for s in new_p]
    outs_s = [jnp.stack(s, axis=0) for s in new_s]
    return (y_p, y_s, *outs_p, *outs_s)


def kernel(x_prompt, x_sample, cache_conv_conformer, cache_k_diff, cache_v_diff, cache_conv_delta, state_delta, cache_conv_ssd, state_ssd, norm_mix_g, w_in, w_out, conf_conv_w, conf_conv_b, conf_ln_g, conf_ln_b, diff_qnorm_g, diff_knorm_g, diff_lambda, diff_subln_g, gdn_conv_w, gdn_A_log, gdn_dt_bias, gdn_onorm_g, ssd_conv_w, ssd_conv_b, ssd_A_log, ssd_dt_bias, ssd_D, ssd_norm_g, norm_ffn_g, moe_w_group, moe_b_group, moe_w_expert, moe_b_expert, moe_w1, moe_w3, moe_w2):
    params = dict(
        norm_mix_g=norm_mix_g, w_in=w_in, w_out=w_out, conf_conv_w=conf_conv_w, conf_conv_b=conf_conv_b,
        conf_ln_g=conf_ln_g, conf_ln_b=conf_ln_b, diff_qnorm_g=diff_qnorm_g, diff_knorm_g=diff_knorm_g,
        diff_lambda=diff_lambda, diff_subln_g=diff_subln_g, gdn_conv_w=gdn_conv_w, gdn_A_log=gdn_A_log,
        gdn_dt_bias=gdn_dt_bias, gdn_onorm_g=gdn_onorm_g, ssd_conv_w=ssd_conv_w, ssd_conv_b=ssd_conv_b,
        ssd_A_log=ssd_A_log, ssd_dt_bias=ssd_dt_bias, ssd_D=ssd_D, ssd_norm_g=ssd_norm_g,
        norm_ffn_g=norm_ffn_g, moe_w_group=moe_w_group, moe_b_group=moe_b_group, moe_w_expert=moe_w_expert,
        moe_b_expert=moe_b_expert, moe_w1=moe_w1, moe_w3=moe_w3, moe_w2=moe_w2)
    return _forward(x_prompt, x_sample, cache_conv_conformer, cache_k_diff, cache_v_diff, cache_conv_delta,
                    state_delta, cache_conv_ssd, state_ssd, params)
```

```python
import functools
import math

import numpy as np
import jax
import jax.numpy as jnp
from jax import lax
from jax.experimental import pallas as pl
from jax.experimental.pallas import tpu as pltpu

F32 = jnp.float32
BF16 = jnp.bfloat16

CHUNK = 64
N_MIXERS = 4
CONV_A = 31
A_GROUPS = 4
H_B = 4
ROPE_THETA = 500000.0
H_C = 4
CONV_SHORT = 4
H_D = 4
G_D = 2
N_D = 128
E_GROUPS = 4
E_PER_GROUP = 4
N_EXPERTS = E_GROUPS * E_PER_GROUP
EPS = 1e-6
NEG_BIG = -1e30

LANES = 128
SUBLANES = 8
CONV_A_PAD = 32
SHORT_PAD = 8
VMEM_LIMIT = 56 * 1024 * 1024
HEADS_PER_GROUP = 2


def _bdot(a, b):
    return jnp.dot(a.astype(BF16), b.astype(BF16), preferred_element_type=F32)


def _bdot_nt(a, b):
    return lax.dot_general(a.astype(BF16), b.astype(BF16), (((1,), (1,)), ((), ())),
                           preferred_element_type=F32)


def _dot3(a, b):
    a_hi = a.astype(BF16)
    a_lo = (a - a_hi.astype(F32)).astype(BF16)
    b_hi = b.astype(BF16)
    b_lo = (b - b_hi.astype(F32)).astype(BF16)
    return (jnp.dot(a_hi, b_hi, preferred_element_type=F32) + jnp.dot(a_hi, b_lo, preferred_element_type=F32)
            + jnp.dot(a_lo, b_hi, preferred_element_type=F32))


def _split_dot(x, m, n_split=3):
    acc = None
    r = x
    for i in range(n_split):
        p = r.astype(BF16)
        d = jnp.dot(p, m, preferred_element_type=F32)
        acc = d if acc is None else acc + d
        if i + 1 < n_split:
            r = r - p.astype(F32)
    return acc


def _split_dot_rhs(m, x, n_split=3):
    acc = None
    r = x
    for i in range(n_split):
        p = r.astype(BF16)
        d = jnp.dot(m, p, preferred_element_type=F32)
        acc = d if acc is None else acc + d
        if i + 1 < n_split:
            r = r - p.astype(F32)
    return acc


def _sigmoid(x):
    return 1.0 / (1.0 + jnp.exp(-x))


def _silu(x):
    return x * _sigmoid(x)


def _softplus(x):
    return jnp.maximum(x, 0.0) + jnp.log1p(jnp.exp(-jnp.abs(x)))


def _tile4(a):
    return jnp.concatenate([a, a, a, a], axis=0)


def _fold4(a, n):
    return a[0:n] + a[n:2 * n] + a[2 * n:3 * n] + a[3 * n:4 * n]


def _cparams(sem):
    return pltpu.CompilerParams(dimension_semantics=sem, vmem_limit_bytes=VMEM_LIMIT)


def _const_spec(shape):
    nd = len(shape)
    return pl.BlockSpec(shape, lambda *_: (0,) * nd)


def _in_proj_kernel(x_ref, g_ref, w_ref, *out_refs, segs):
    x = x_ref[...]
    ms = jnp.mean(x * x, axis=-1, keepdims=True)
    xn = (x * lax.rsqrt(ms + EPS) * g_ref[...]).astype(BF16)
    off = 0
    for o_ref, n in zip(out_refs, segs):
        o_ref[...] = jnp.dot(xn, w_ref[:, off:off + n], preferred_element_type=F32)
        off += n


def _in_proj(x2d, g, w_pad, segs):
    t, d = x2d.shape
    tm = min(512, t)
    n_all = sum(segs)
    return pl.pallas_call(
        functools.partial(_in_proj_kernel, segs=segs),
        out_shape=[jax.ShapeDtypeStruct((t, n), F32) for n in segs],
        grid=(t // tm,),
        in_specs=[pl.BlockSpec((tm, d), lambda i: (i, 0)),
                  _const_spec((1, d)),
                  _const_spec((d, n_all))],
        out_specs=[pl.BlockSpec((tm, n), lambda i: (i, 0)) for n in segs],
        compiler_params=_cparams(("parallel",)),
        name="in_proj",
    )(x2d, g, w_pad)


def _conf_kernel(u_ref, cache_ref, w_ref, b_ref, g_ref, beta_ref, pm_ref, y_ref, tail_ref, abuf, shifted,
                 *, tl, wg):
    t = pl.program_id(1)

    @pl.when(t == 0)
    def _():
        abuf[0:CONV_A_PAD, :] = cache_ref[0]

    u = u_ref[0]
    a = u[:, :wg] * _sigmoid(u[:, wg:])
    abuf[CONV_A_PAD:CONV_A_PAD + tl, :] = a
    span = tl + CONV_A_PAD - SUBLANES
    for r in range(1, SUBLANES):
        shifted[r - 1] = abuf[r:r + span, :]
    acc = jnp.zeros((tl, wg), F32) + b_ref[...]
    base = CONV_A_PAD - (CONV_A - 1)
    for j in range(CONV_A):
        q, r = divmod(base + j, SUBLANES)
        win = abuf[SUBLANES * q:SUBLANES * q + tl, :] if r == 0 else shifted[r - 1, SUBLANES * q:SUBLANES * q + tl, :]
        acc = acc + w_ref[j:j + 1, :] * win
    pm = pm_ref[...]
    mu = _split_dot(acc, pm)
    dlt = acc - mu
    var = _split_dot(dlt * dlt, pm)
    cn = dlt * lax.rsqrt(var + EPS) * g_ref[...] + beta_ref[...]
    y_ref[0] = _silu(cn).astype(BF16)
    tail = abuf[tl:tl + CONV_A_PAD, :]
    tail_ref[0] = tail
    abuf[0:CONV_A_PAD, :] = tail


def _conformer(u_a, cache_pad, w_pad, b, g, beta, pm):
    bsz, l, two_wg = u_a.shape
    wg = two_wg // 2
    tl = min(512, l)
    return pl.pallas_call(
        functools.partial(_conf_kernel, tl=tl, wg=wg),
        out_shape=[jax.ShapeDtypeStruct((bsz, l, wg), BF16),
                   jax.ShapeDtypeStruct((bsz, CONV_A_PAD, wg), F32)],
        grid=(bsz, l // tl),
        in_specs=[pl.BlockSpec((1, tl, two_wg), lambda b_, t_: (b_, t_, 0)),
                  pl.BlockSpec((1, CONV_A_PAD, wg), lambda b_, t_: (b_, 0, 0)),
                  _const_spec((CONV_A_PAD, wg)),
                  _const_spec((1, wg)), _const_spec((1, wg)), _const_spec((1, wg)),
                  _const_spec((wg, wg))],
        out_specs=[pl.BlockSpec((1, tl, wg), lambda b_, t_: (b_, t_, 0)),
                   pl.BlockSpec((1, CONV_A_PAD, wg), lambda b_, t_: (b_, 0, 0))],
        scratch_shapes=[pltpu.VMEM((CONV_A_PAD + tl, wg), F32),
                        pltpu.VMEM((SUBLANES - 1, tl + CONV_A_PAD - SUBLANES, wg), F32)],
        compiler_params=_cparams(("parallel", "arbitrary")),
        name="conformer",
    )(u_a, cache_pad, w_pad, b, g, beta, pm)


def _attn_prep_kernel(qkv_ref, cos_ref, sin_ref, qg_ref, kg_ref, pm_ref, eye_ref,
                      q_ref, kt_ref, vb_ref, kn_ref, vn_ref, *, wg, dqk):
    qkv = qkv_ref[0]
    cos = cos_ref[...]
    sin = sin_ref[...]
    pm = pm_ref[...]
    lane = lax.broadcasted_iota(jnp.int32, (1, wg), 1)
    low = (lane % dqk) < (dqk // 8)

    def norm_rope(x, g):
        ms = _split_dot(x * x, pm)
        xn = x * lax.rsqrt(ms + EPS) * g
        rot = jnp.where(low, pltpu.roll(xn, wg - dqk // 8, 1), pltpu.roll(xn, dqk // 8, 1))
        return xn * cos + rot * sin

    q = norm_rope(qkv[:, :wg], qg_ref[...])
    k = norm_rope(qkv[:, wg:2 * wg], kg_ref[...])
    v = qkv[:, 2 * wg:]
    q_ref[0] = q.astype(BF16)
    kn_ref[0] = k
    vn_ref[0] = v
    dv = 2 * dqk
    pad = jnp.where(lax.broadcasted_iota(jnp.int32, (v.shape[0], dv), 1) == 0, 1.0, 0.0)
    vb_ref[0] = jnp.concatenate(
        [blk for h in range(H_B) for blk in (v[:, dv * h:dv * (h + 1)], pad)], axis=1).astype(BF16)
    kt_ref[0] = _bdot_nt(eye_ref[...], k).astype(BF16)


def _attn_prep(qkv, cos_t, sin_t, qg, kg, pm, eye, dqk):
    bsz, l, w3 = qkv.shape
    wg = w3 // 3
    tl = min(512, l)
    kern = functools.partial(_attn_prep_kernel, wg=wg, dqk=dqk)
    return pl.pallas_call(
        kern,
        out_shape=[jax.ShapeDtypeStruct((bsz, l, wg), BF16),
                   jax.ShapeDtypeStruct((bsz, wg, l), BF16),
                   jax.ShapeDtypeStruct((bsz, l, 2 * wg), BF16),
                   jax.ShapeDtypeStruct((bsz, l, wg), F32),
                   jax.ShapeDtypeStruct((bsz, l, wg), F32)],
        grid=(l // tl, bsz),
        in_specs=[pl.BlockSpec((1, tl, w3), lambda t_, b_: (b_, t_, 0)),
                  pl.BlockSpec((tl, wg), lambda t_, b_: (t_, 0)),
                  pl.BlockSpec((tl, wg), lambda t_, b_: (t_, 0)),
                  _const_spec((1, wg)), _const_spec((1, wg)),
                  _const_spec((wg, wg)), _const_spec((wg, wg))],
        out_specs=[pl.BlockSpec((1, tl, wg), lambda t_, b_: (b_, t_, 0)),
                   pl.BlockSpec((1, wg, tl), lambda t_, b_: (b_, 0, t_)),
                   pl.BlockSpec((1, tl, 2 * wg), lambda t_, b_: (b_, t_, 0)),
                   pl.BlockSpec((1, tl, wg), lambda t_, b_: (b_, t_, 0)),
                   pl.BlockSpec((1, tl, wg), lambda t_, b_: (b_, t_, 0))],
        compiler_params=_cparams(("parallel", "parallel")),
        name="attn_prep",
    )(qkv, cos_t, sin_t, qg, kg, pm, eye)


def _kt_kernel(k_ref, eye_ref, kt_ref):
    kt_ref[0] = _bdot_nt(eye_ref[...], k_ref[0]).astype(BF16)


def _transpose_keys(k, eye):
    bsz, p, w = k.shape
    tp = min(512, p)
    return pl.pallas_call(
        _kt_kernel,
        out_shape=jax.ShapeDtypeStruct((bsz, w, p), BF16),
        grid=(bsz, p // tp),
        in_specs=[pl.BlockSpec((1, tp, w), lambda b_, t_: (b_, t_, 0)), _const_spec((w, w))],
        out_specs=pl.BlockSpec((1, w, tp), lambda b_, t_: (b_, 0, t_)),
        compiler_params=_cparams(("parallel", "parallel")),
        name="transpose_keys",
    )(k, eye)


def _flash_kernel(*refs, tq, tkb, tkp, n_past, nq, dqk, dv, lam_init):
    if n_past:
        q_ref, kt_ref, v_ref, ktp_ref, vp_ref, lam_ref, sg_ref, o_ref = refs
    else:
        q_ref, kt_ref, v_ref, lam_ref, sg_ref, o_ref = refs
    i = pl.program_id(1)
    lp = lam_ref[...]
    lam = (jnp.exp(jnp.sum(lp[0:1] * lp[1:2], axis=-1, keepdims=True))
           - jnp.exp(jnp.sum(lp[2:3] * lp[3:4], axis=-1, keepdims=True)) + lam_init)
    sg = sg_ref[...]
    row_c = lax.broadcasted_iota(jnp.int32, (tq, tq), 0) // CHUNK
    col_c = lax.broadcasted_iota(jnp.int32, (tq, tq), 1) // CHUNK
    diag_mask = col_c <= row_c
    hw = 2 * dqk
    vw = 2 * dv
    c2 = (dqk ** -0.5) * math.log2(math.e)

    def update(qm, ktm, vt, carry, mask):
        m, acc = carry
        s = jnp.dot(qm, ktm, preferred_element_type=F32)
        if mask is not None:
            s = jnp.where(mask, s, NEG_BIG)
        m_new = jnp.maximum(m, jnp.max(s, axis=-1, keepdims=True))
        alpha = jnp.exp2((m - m_new) * c2)
        p = jnp.exp2((s - m_new) * c2)
        acc = alpha * acc + jnp.dot(p.astype(BF16), vt, preferred_element_type=F32)
        return m_new, acc

    heads = range(H_B)
    rows = [slice(hw * h, hw * (h + 1)) for h in heads]
    cols = [slice(vw * h, vw * (h + 1)) for h in heads]
    qs = []
    for h in heads:
        qh = q_ref[0, :, rows[h]]
        qs.append((qh[:, :dqk], qh[:, dqk:]))

    def finish(o):
        ms = jnp.mean(o * o, axis=-1, keepdims=True)
        return o * lax.rsqrt(ms + EPS) * sg * (1.0 - lam_init)

    if nq == 1:
        outs = []
        for h in heads:
            kts = [ktp_ref[0, rows[h], :]] if n_past else []
            vts = [vp_ref[0, :, cols[h]]] if n_past else []
            kt = jnp.concatenate(kts + [kt_ref[0, rows[h], :]], axis=1)
            vt = jnp.concatenate(vts + [v_ref[0, :, cols[h]]], axis=0)
            n_old = kt.shape[1] - tq
            vis = jnp.concatenate([jnp.full((tq, n_old), True), diag_mask], axis=1) if n_old else diag_mask

            def probs(qm, ktm):
                s = jnp.dot(qm, ktm, preferred_element_type=F32) * (dqk ** -0.5)
                s = jnp.where(vis, s, NEG_BIG)
                e = jnp.exp(s - jnp.max(s, axis=-1, keepdims=True))
                return e / jnp.sum(e, axis=-1, keepdims=True)

            pd = probs(qs[h][0], kt[:dqk]) - lam * probs(qs[h][1], kt[dqk:])
            outs.append(finish(jnp.dot(pd.astype(BF16), vt, preferred_element_type=F32)[:, :dv]))
        o_ref[0] = jnp.concatenate(outs, axis=1).astype(BF16)
        return

    def tile_group(carries, group, kt_of, vt_of, mask):
        chains = [(h, c) for h in group for c in range(2)]
        kts = {h: kt_of(rows[h]) for h in group}
        vts = {h: vt_of(cols[h]) for h in group}
        s = [jnp.dot(qs[h][c], kts[h][dqk * c:dqk * (c + 1)], preferred_element_type=F32) for h, c in chains]
        if mask is not None:
            s = [jnp.where(mask, x, NEG_BIG) for x in s]
        m_old = [carries[h][c][0] for h, c in chains]
        m_new = [jnp.maximum(m, jnp.max(x, axis=-1, keepdims=True)) for m, x in zip(m_old, s)]
        p = [jnp.exp2((x - m) * c2).astype(BF16) for x, m in zip(s, m_new)]
        pv = [jnp.dot(x, vts[h], preferred_element_type=F32) for x, (h, c) in zip(p, chains)]
        acc = [jnp.exp2((mo - mn) * c2) * carries[h][c][1] + y
               for mo, mn, y, (h, c) in zip(m_old, m_new, pv, chains)]
        return [((m_new[2 * g], acc[2 * g]), (m_new[2 * g + 1], acc[2 * g + 1])) for g in range(len(group))]

    def tile(carries, kt_of, vt_of, mask):
        out = []
        for g0 in range(0, H_B, HEADS_PER_GROUP):
            out += tile_group(carries, list(range(g0, g0 + HEADS_PER_GROUP)), kt_of, vt_of, mask)
        return tuple(out)

    init = (jnp.full((tq, 1), NEG_BIG, F32), jnp.zeros((tq, vw), F32))
    carries = tuple((init, init) for _ in heads)
    for j in range(n_past):
        carries = tile(carries, lambda r: ktp_ref[0, r, j * tkp:(j + 1) * tkp],
                       lambda c: vp_ref[0, j * tkp:(j + 1) * tkp, c], None)

    start = i * tq
    n_big = start // tkb if tkb > tq else 0

    def keys(off, width):
        return (lambda r: kt_ref[0, r, pl.ds(off, width)]), (lambda c: v_ref[0, pl.ds(off, width), c])

    def body_big(j, carries):
        return tile(carries, *keys(pl.multiple_of(j * tkb, tkb), tkb), None)

    n_small = (start - n_big * tkb) // tq

    def body_small(j, carries):
        bump = jnp.where(j == n_small, 0, tq // CHUNK)
        return tile(carries, *keys(pl.multiple_of(n_big * tkb + j * tq, tq), tq), col_c <= row_c + bump)

    if tkb > tq:
        carries = lax.fori_loop(0, n_big, body_big, carries)
    carries = lax.fori_loop(0, n_small + 1, body_small, carries)

    outs = []
    for h in heads:
        (_, a0), (_, a1) = carries[h]
        outs.append(finish(a0[:, :dv] * (1.0 / a0[:, dv:dv + 1]) - lam * (a1[:, :dv] * (1.0 / a1[:, dv:dv + 1]))))
    o_ref[0] = jnp.concatenate(outs, axis=1).astype(BF16)


def _flash(q, kt, v1, ktp, vp1, lam_p, sg, lam_init, dqk, dv):
    bsz, l, wg = q.shape
    vw_all = v1.shape[2]
    tq = min(512, l)
    tkb = min(1024, l)
    nq = l // tq
    n_past = 0
    tkp = 0
    args = [q, kt, v1]
    in_specs = [pl.BlockSpec((1, tq, wg), lambda b_, i_: (b_, i_, 0)),
                pl.BlockSpec((1, wg, l), lambda b_, i_: (b_, 0, 0)),
                pl.BlockSpec((1, l, vw_all), lambda b_, i_: (b_, 0, 0))]
    if ktp is not None:
        p = ktp.shape[2]
        tkp = min(2048, p)
        n_past = p // tkp
        args += [ktp, vp1]
        in_specs += [pl.BlockSpec((1, wg, p), lambda b_, i_: (b_, 0, 0)),
                     pl.BlockSpec((1, p, vw_all), lambda b_, i_: (b_, 0, 0))]
    args += [lam_p, sg]
    in_specs += [_const_spec(lam_p.shape), _const_spec(sg.shape)]
    kern = functools.partial(_flash_kernel, tq=tq, tkb=tkb, tkp=tkp, n_past=n_past, nq=nq, dqk=dqk, dv=dv,
                             lam_init=lam_init)
    return pl.pallas_call(
        kern,
        out_shape=jax.ShapeDtypeStruct((bsz, l, wg), BF16),
        grid=(bsz, nq),
        in_specs=in_specs,
        out_specs=pl.BlockSpec((1, tq, wg), lambda b_, i_: (b_, i_, 0)),
        compiler_params=_cparams(("parallel", "parallel")),
        name="flash_diff_attn",
    )(*args)


def _short_conv(x_ref, hist_ref, w_ref, xbuf, t, tl):
    @pl.when(t == 0)
    def _():
        xbuf[0:SHORT_PAD, :] = hist_ref[0]

    xbuf[SHORT_PAD:SHORT_PAD + tl, :] = x_ref[0]
    base = SHORT_PAD - (CONV_SHORT - 1)
    acc = w_ref[0:1, :] * xbuf[base:base + tl, :]
    for j in range(1, CONV_SHORT):
        acc = acc + w_ref[j:j + 1, :] * xbuf[base + j:base + j + tl, :]
    xbuf[0:SHORT_PAD, :] = xbuf[tl:tl + SHORT_PAD, :]
    return acc


def _bd_masks(n):
    r = lax.broadcasted_iota(jnp.int32, (n, n), 0)
    c = lax.broadcasted_iota(jnp.int32, (n, n), 1)
    head = (r // CHUNK) == (c // CHUNK)
    incl = head & (c <= r)
    strict = head & (c < r)
    return head, incl, strict, r == c


def _expand_col(x4, lane0, n):
    r = lax.broadcasted_iota(jnp.int32, (n, LANES), 0) // CHUNK
    ln = lax.broadcasted_iota(jnp.int32, (n, LANES), 1)
    return jnp.sum(jnp.where(ln == r + lane0, _tile4(x4), 0.0), axis=-1, keepdims=True)


def _expand_row(x1, lane0, n):
    r = lax.broadcasted_iota(jnp.int32, (n, LANES), 0) // CHUNK
    ln = lax.broadcasted_iota(jnp.int32, (n, LANES), 1)
    return jnp.sum(jnp.where(ln == r + lane0, jnp.broadcast_to(x1, (n, LANES)), 0.0), axis=-1, keepdims=True)


def _decay_matrix(cum_r, incl, n):
    cm = jnp.broadcast_to(cum_r, (n, n))
    return jnp.where(incl, jnp.exp(jnp.minimum(cm - cm.T, 0.0)), 0.0)


def _gdn_kernel(qkv_ref, z_ref, sm_ref, hist_ref, s0_ref, cw_ref, alog_ref, dtb_ref, og_ref, ones_ref, tril_ref,
                y_ref, sout_ref, xbuf, s_scr, q_s, k_s, v_s, b_s, g_s, o_s, *, tl, wg):
    t = pl.program_id(1)
    n = wg

    @pl.when(t == 0)
    def _():
        s_scr[...] = s0_ref[0]

    c = _silu(_short_conv(qkv_ref, hist_ref, cw_ref, xbuf, t, tl))
    ones_bd = ones_ref[...]
    q = c[:, :wg]
    k = c[:, wg:2 * wg]
    dk = wg // H_C
    q_s[...] = q * lax.rsqrt(_split_dot(q * q, ones_bd) + EPS) * (dk ** -0.5)
    k_s[...] = k * lax.rsqrt(_split_dot(k * k, ones_bd) + EPS)
    v_s[...] = c[:, 2 * wg:]
    sm = sm_ref[0]
    b_s[...] = _sigmoid(sm)
    g_s[...] = -jnp.exp(alog_ref[...]) * _softplus(sm + dtb_ref[...])

    head, incl, strict, eye = _bd_masks(n)
    eye_f = jnp.where(eye, 1.0, 0.0)
    tril = tril_ref[...]

    sls = [slice(ci * CHUNK, (ci + 1) * CHUNK) for ci in range(tl // CHUNK)]
    cum = [_split_dot_rhs(tril, g_s[sl, :]) for sl in sls]
    beta_r = [_expand_col(b_s[sl, :], 0, n) for sl in sls]
    cum_r = [_expand_col(cm, H_C, n) for cm in cum]
    tot_r = [_expand_row(cm[CHUNK - 1:CHUNK, :], H_C, n) for cm in cum]
    kx = [jnp.where(head, _tile4(k_s[sl, :]), 0.0) for sl in sls]
    qx = [jnp.where(head, _tile4(q_s[sl, :]), 0.0) for sl in sls]
    vx = [jnp.where(head, _tile4(v_s[sl, :]), 0.0) for sl in sls]
    dm = [_decay_matrix(cr, incl, n) for cr in cum_r]
    a = [jnp.where(strict, br * _bdot_nt(kc, kc) * dc, 0.0) for br, kc, dc in zip(beta_r, kx, dm)]
    x = [eye_f - ac for ac in a]
    p = a
    for _ in range(int(math.log2(CHUNK)) - 2):
        p = [_bdot(pc, pc) for pc in p]
        x = [xc + _bdot(xc, pc) for xc, pc in zip(x, p)]
    x = [xc + _bdot(xc, eye_f - xc - _dot3(ac, xc)) for xc, ac in zip(x, a)]
    ecum = [jnp.exp(cr) for cr in cum_r]
    u = [_dot3(xc, br * vc) for xc, br, vc in zip(x, beta_r, vx)]
    w = [_dot3(xc, (br * ec) * kc) for xc, br, ec, kc in zip(x, beta_r, ecum, kx)]
    qe = [qc * ec for qc, ec in zip(qx, ecum)]
    qkd = [_bdot_nt(qc, kc) * dc for qc, kc, dc in zip(qx, kx, dm)]
    kdt = [(kc * jnp.exp(tr - cr)).T for kc, tr, cr in zip(kx, tot_r, cum_r)]
    s = s_scr[...]
    for ci, sl in enumerate(sls):
        vn = u[ci] - _bdot(w[ci], s)
        o = _bdot(qe[ci], s) + _bdot(qkd[ci], vn)
        o_s[sl, :] = _fold4(o, CHUNK)
        s = s * jnp.exp(tot_r[ci]) + _bdot(kdt[ci], vn)
    s_scr[...] = s
    o = o_s[...]
    ms = _split_dot(o * o, ones_bd) * (1.0 / dk)
    y_ref[0] = (o * lax.rsqrt(ms + EPS) * og_ref[...] * _silu(z_ref[0])).astype(BF16)
    sout_ref[0] = s_scr[...]


def _gdn(qkv, z, small, hist, s0_bd, cw, alog, dtb, og, ones_bd, tril):
    bsz, l, w3 = qkv.shape
    wg = w3 // 3
    tl = min(256, l)
    kern = functools.partial(_gdn_kernel, tl=tl, wg=wg)
    bt = lambda b_, t_: (b_, t_, 0)
    b0 = lambda b_, t_: (b_, 0, 0)
    return pl.pallas_call(
        kern,
        out_shape=[jax.ShapeDtypeStruct((bsz, l, wg), BF16),
                   jax.ShapeDtypeStruct((bsz, wg, wg), F32)],
        grid=(bsz, l // tl),
        in_specs=[pl.BlockSpec((1, tl, w3), bt),
                  pl.BlockSpec((1, tl, wg), bt),
                  pl.BlockSpec((1, tl, LANES), bt),
                  pl.BlockSpec((1, SHORT_PAD, w3), b0),
                  pl.BlockSpec((1, wg, wg), b0),
                  _const_spec((SHORT_PAD, w3)),
                  _const_spec((1, LANES)), _const_spec((1, LANES)), _const_spec((1, wg)),
                  _const_spec((wg, wg)), _const_spec((CHUNK, CHUNK))],
        out_specs=[pl.BlockSpec((1, tl, wg), bt),
                   pl.BlockSpec((1, wg, wg), b0)],
        scratch_shapes=[pltpu.VMEM((SHORT_PAD + tl, w3), F32),
                        pltpu.VMEM((wg, wg), F32),
                        pltpu.VMEM((tl, wg), F32), pltpu.VMEM((tl, wg), F32), pltpu.VMEM((tl, wg), F32),
                        pltpu.VMEM((tl, LANES), F32), pltpu.VMEM((tl, LANES), F32),
                        pltpu.VMEM((tl, wg), F32)],
        compiler_params=_cparams(("parallel", "arbitrary")),
        name="gated_deltanet",
    )(qkv, z, small, hist, s0_bd, cw, alog, dtb, og, ones_bd, tril)


def _ssd_kernel(xbc_ref, z_ref, sm_ref, hist_ref, h0_ref, cw_ref, cb_ref, alog_ref, dtb_ref, dsk_ref, ng_ref,
                tril_ref, y_ref, hout_ref, xbuf, h_scr, x_s, b_s, c_s, dt_s, da_s, y_s, *, tl, wg):
    t = pl.program_id(1)
    n = wg

    @pl.when(t == 0)
    def _():
        h_scr[...] = h0_ref[0]

    c = _silu(_short_conv(xbc_ref, hist_ref, cw_ref, xbuf, t, tl) + cb_ref[...])
    xs = c[:, :wg]
    x_s[...] = xs
    b_s[...] = c[:, wg:wg + G_D * N_D]
    c_s[...] = c[:, wg + G_D * N_D:]
    dt = _softplus(sm_ref[0] + dtb_ref[...])
    dt_s[...] = dt
    da_s[...] = -jnp.exp(alog_ref[...]) * dt

    head, incl, _, _ = _bd_masks(n)
    tril = tril_ref[...]
    lane0 = H_C + H_C

    rep = H_D // G_D

    def per_head(a):
        return jnp.concatenate([a[:, N_D * (hh // rep):N_D * (hh // rep + 1)] for hh in range(H_D)], axis=0)

    sls = [slice(ci * CHUNK, (ci + 1) * CHUNK) for ci in range(tl // CHUNK)]
    acs = [_split_dot_rhs(tril, da_s[sl, :]) for sl in sls]
    dt_r = [_expand_col(dt_s[sl, :], lane0, n) for sl in sls]
    acs_r = [_expand_col(ac, lane0, n) for ac in acs]
    tot_r = [_expand_row(ac[CHUNK - 1:CHUNK, :], lane0, n) for ac in acs]
    xdt = [jnp.where(head, _tile4(x_s[sl, :]), 0.0) * dr for sl, dr in zip(sls, dt_r)]
    cn = [per_head(c_s[sl, :]) for sl in sls]
    bn = [per_head(b_s[sl, :]) for sl in sls]
    scores = [_bdot_nt(cc, bc) * _decay_matrix(ar, incl, n) for cc, bc, ar in zip(cn, bn, acs_r)]
    y_diag = [_bdot(sc, xc) for sc, xc in zip(scores, xdt)]
    s_chunk = [_bdot(xc.T, bc * jnp.exp(tr - ar)) for xc, bc, tr, ar in zip(xdt, bn, tot_r, acs_r)]
    c_dec = [cc * jnp.exp(ar) for cc, ar in zip(cn, acs_r)]
    hs = h_scr[...]
    for ci, sl in enumerate(sls):
        y_bd = y_diag[ci] + jnp.where(head, _bdot_nt(c_dec[ci], hs), 0.0)
        y_s[sl, :] = _fold4(y_bd, CHUNK)
        hs = hs * jnp.exp(tot_r[ci]) + s_chunk[ci]
    h_scr[...] = hs
    y = (y_s[...] + dsk_ref[...] * xs) * _silu(z_ref[0])
    gw = wg // G_D
    parts = []
    for g in range(G_D):
        yg = y[:, gw * g:gw * (g + 1)]
        ms = jnp.mean(yg * yg, axis=-1, keepdims=True)
        parts.append(yg * lax.rsqrt(ms + EPS))
    y_ref[0] = (jnp.concatenate(parts, axis=1) * ng_ref[...]).astype(BF16)
    hout_ref[0] = h_scr[...]


def _ssd(xbc, z, small, hist, h0, cw, cb, alog, dtb, dsk, ng, tril):
    bsz, l, wc = xbc.shape
    wg = z.shape[2]
    tl = min(256, l)
    kern = functools.partial(_ssd_kernel, tl=tl, wg=wg)
    bt = lambda b_, t_: (b_, t_, 0)
    b0 = lambda b_, t_: (b_, 0, 0)
    return pl.pallas_call(
        kern,
        out_shape=[jax.ShapeDtypeStruct((bsz, l, wg), BF16),
                   jax.ShapeDtypeStruct((bsz, wg, N_D), F32)],
        grid=(bsz, l // tl),
        in_specs=[pl.BlockSpec((1, tl, wc), bt),
                  pl.BlockSpec((1, tl, wg), bt),
                  pl.BlockSpec((1, tl, LANES), bt),
                  pl.BlockSpec((1, SHORT_PAD, wc), b0),
                  pl.BlockSpec((1, wg, N_D), b0),
                  _const_spec((SHORT_PAD, wc)), _const_spec((1, wc)),
                  _const_spec((1, LANES)), _const_spec((1, LANES)), _const_spec((1, wg)), _const_spec((1, wg)),
                  _const_spec((CHUNK, CHUNK))],
        out_specs=[pl.BlockSpec((1, tl, wg), bt),
                   pl.BlockSpec((1, wg, N_D), b0)],
        scratch_shapes=[pltpu.VMEM((SHORT_PAD + tl, wc), F32),
                        pltpu.VMEM((wg, N_D), F32),
                        pltpu.VMEM((tl, wg), F32), pltpu.VMEM((tl, G_D * N_D), F32), pltpu.VMEM((tl, G_D * N_D), F32),
                        pltpu.VMEM((tl, LANES), F32), pltpu.VMEM((tl, LANES), F32),
                        pltpu.VMEM((tl, wg), F32)],
        compiler_params=_cparams(("parallel", "arbitrary")),
        name="ssd",
    )(xbc, z, small, hist, h0, cw, cb, alog, dtb, dsk, ng, tril)


def _route(logits):
    lane = lax.broadcasted_iota(jnp.int32, logits.shape, 1).astype(F32)
    big = float(LANES)
    lg = jnp.where(lane < E_GROUPS, logits, NEG_BIG)
    mg = jnp.max(lg, axis=-1, keepdims=True)
    sg = jnp.sum(jnp.exp(lg - mg), axis=-1, keepdims=True)
    gidx = jnp.min(jnp.where(lg == mg, lane, big), axis=-1, keepdims=True)
    w_grp = 1.0 / sg
    lo = E_GROUPS + E_PER_GROUP * gidx
    sel = (lane >= lo) & (lane < lo + E_PER_GROUP)
    le = jnp.where(sel, logits, NEG_BIG)
    me = jnp.max(le, axis=-1, keepdims=True)
    pe = jnp.where(sel, jnp.exp(le - me), 0.0)
    p_in = pe / jnp.sum(pe, axis=-1, keepdims=True)
    v1 = jnp.max(p_in, axis=-1, keepdims=True)
    i1 = jnp.min(jnp.where(sel & (p_in == v1), lane, big), axis=-1, keepdims=True)
    rest = sel & (lane != i1)
    p2 = jnp.where(rest, p_in, -1.0)
    v2 = jnp.max(p2, axis=-1, keepdims=True)
    i2 = jnp.min(jnp.where(rest & (p2 == v2), lane, big), axis=-1, keepdims=True)
    den = v1 + v2
    gate = jnp.where(lane == i1, v1 / den, 0.0) + jnp.where(lane == i2, v2 / den, 0.0)
    return gate * w_grp


def _moe_kernel(x_ref, ya_ref, yb_ref, yc_ref, yd_ref, wo_ref, g_ref, wr_ref, br_ref, w13_ref, w2_ref, o_ref, act_s,
                *, f, ne, wg):
    h = x_ref[...]
    for i, y_ref in enumerate((ya_ref, yb_ref, yc_ref, yd_ref)):
        h = h + jnp.dot(y_ref[...], wo_ref[wg * i:wg * (i + 1), :], preferred_element_type=F32)
    ms = jnp.mean(h * h, axis=-1, keepdims=True)
    xn = (h * lax.rsqrt(ms + EPS) * g_ref[...]).astype(BF16)
    gate = _route(jnp.dot(xn, wr_ref[...], preferred_element_type=F32) + br_ref[...])
    for e in range(ne):
        hid = jnp.dot(xn, w13_ref[e], preferred_element_type=F32)
        act = _silu(hid[:, :f]) * hid[:, f:]
        act_s[:, f * e:f * (e + 1)] = (act * gate[:, E_GROUPS + e:E_GROUPS + e + 1]).astype(BF16)
    o_ref[...] = h + jnp.dot(act_s[...], w2_ref[...], preferred_element_type=F32)


def _out_proj_moe(x2d, ys, w_out, g, wr, br, w13, w2s):
    t, d = x2d.shape
    wg = ys[0].shape[1]
    ne, _, f2 = w13.shape
    f = f2 // 2
    tm = min(512, t)
    row = lambda i: (i, 0)
    resident = dict(pipeline_mode=pl.Buffered(1))
    return pl.pallas_call(
        functools.partial(_moe_kernel, f=f, ne=ne, wg=wg),
        out_shape=jax.ShapeDtypeStruct((t, d), F32),
        grid=(t // tm,),
        in_specs=[pl.BlockSpec((tm, d), row)] + [pl.BlockSpec((tm, wg), row)] * 4 + [
                  pl.BlockSpec(w_out.shape, lambda i: (0, 0), **resident),
                  _const_spec((1, d)), _const_spec((d, LANES)), _const_spec((1, LANES)),
                  pl.BlockSpec((ne, d, f2), lambda i: (0, 0, 0), **resident),
                  pl.BlockSpec((ne * f, d), lambda i: (0, 0), **resident)],
        out_specs=pl.BlockSpec((tm, d), row),
        scratch_shapes=[pltpu.VMEM((tm, ne * f), BF16)],
        compiler_params=_cparams(("parallel",)),
        name="out_proj_moe",
    )(x2d, *ys, w_out, g, wr, br, w13, w2s)


def _block_diag_const(n, blk, val):
    r = np.arange(n)
    return jnp.asarray(np.where((r[:, None] // blk) == (r[None, :] // blk), val, 0.0), dtype=BF16)


def _rope_tables(pos, wg, dqk):
    rot = dqk // 4
    half = rot // 2
    inv = jnp.exp(jnp.arange(half, dtype=F32) * (-2.0 / rot) * math.log(ROPE_THETA))
    ang = pos.astype(F32)[:, None] * inv[None, :]
    cos, sin = jnp.cos(ang), jnp.sin(ang)
    l = pos.shape[0]
    ones = jnp.ones((l, dqk - rot), F32)
    zeros = jnp.zeros((l, dqk - rot), F32)
    cos_d = jnp.concatenate([cos, cos, ones], axis=1)
    sin_d = jnp.concatenate([-sin, sin, zeros], axis=1)
    reps = wg // dqk
    return jnp.tile(cos_d, (1, reps)), jnp.tile(sin_d, (1, reps))


def _pad_lanes(v, lane0):
    out = jnp.zeros((1, LANES), F32)
    return out.at[0, lane0:lane0 + v.shape[0]].set(v.astype(F32))


def _layer_params(l, p, wg, d):
    sizes = (2 * wg, 3 * wg, 3 * wg, wg, H_C, H_C, wg, wg + 2 * G_D * N_D, H_D)
    cuts = np.cumsum((0,) + sizes)
    w_in = p['w_in'][l]
    seg = lambda i: w_in[:, cuts[i]:cuts[i + 1]]
    n_small = H_C + H_C + H_D
    w_pad = jnp.concatenate([seg(0), seg(1), seg(2), seg(3), seg(6), seg(7), seg(4), seg(5), seg(8),
                             jnp.zeros((d, LANES - n_small), F32)], axis=1).astype(BF16)
    segs = (sizes[0], sizes[1], sizes[2], sizes[3], sizes[6], sizes[7], LANES)
    dqk = wg // (2 * H_B)
    row = lambda v: v.astype(F32).reshape(1, -1)
    wc = sizes[7]
    lp = dict(
        segs=segs, w_in=w_pad, norm_mix_g=row(p['norm_mix_g'][l]),
        w_out=p['w_out'][l].astype(BF16),
        conf_w=jnp.concatenate([p['conf_conv_w'][l], jnp.zeros((CONV_A_PAD - CONV_A, wg), F32)], axis=0),
        conf_b=row(p['conf_conv_b'][l]), conf_g=row(p['conf_ln_g'][l]), conf_beta=row(p['conf_ln_b'][l]),
        qg=row(jnp.tile(p['diff_qnorm_g'][l], wg // dqk)), kg=row(jnp.tile(p['diff_knorm_g'][l], wg // dqk)),
        lam_p=p['diff_lambda'][l].astype(F32), subln_g=row(p['diff_subln_g'][l]),
        gdn_w=jnp.concatenate([p['gdn_conv_w'][l], jnp.zeros((SHORT_PAD - CONV_SHORT, 3 * wg), F32)], axis=0),
        gdn_alog=_pad_lanes(p['gdn_A_log'][l], H_C), gdn_dtb=_pad_lanes(p['gdn_dt_bias'][l], H_C),
        gdn_og=row(jnp.tile(p['gdn_onorm_g'][l], H_C)),
        ssd_w=jnp.concatenate([p['ssd_conv_w'][l], jnp.zeros((SHORT_PAD - CONV_SHORT, wc), F32)], axis=0),
        ssd_b=row(p['ssd_conv_b'][l]),
        ssd_alog=_pad_lanes(p['ssd_A_log'][l], 2 * H_C), ssd_dtb=_pad_lanes(p['ssd_dt_bias'][l], 2 * H_C),
        ssd_dsk=row(jnp.repeat(p['ssd_D'][l], wg // H_D)), ssd_ng=row(p['ssd_norm_g'][l]),
        norm_ffn_g=row(p['norm_ffn_g'][l]),
        moe_wr=jnp.concatenate([p['moe_w_group'][l], p['moe_w_expert'][l],
                                jnp.zeros((d, LANES - E_GROUPS - N_EXPERTS), F32)], axis=1).astype(BF16),
        moe_br=jnp.concatenate([p['moe_b_group'][l], p['moe_b_expert'][l],
                                jnp.zeros((LANES - E_GROUPS - N_EXPERTS,), F32)]).reshape(1, LANES),
        moe_w13=jnp.concatenate([p['moe_w1'][l], p['moe_w3'][l]], axis=-1).astype(BF16),
        moe_w2=p['moe_w2'][l].astype(BF16).reshape(-1, d),
    )
    return lp


def _trunk_layer(x, lp, consts, pos, lam_init, buf_a, k_past, v_past, buf_c, s_c, buf_d, h_d):
    bsz, l, d = x.shape
    wg = d // N_MIXERS
    dqk = wg // (2 * H_B)
    dv = wg // H_B
    assert wg // H_C == CHUNK and wg // H_D == CHUNK and H_C == H_D == N_MIXERS
    assert l % CHUNK == 0 and l >= CONV_A_PAD
    x2d = x.reshape(bsz * l, d)
    u_a, u_b, u_c, u_cz, u_dz, u_d, u_s = _in_proj(x2d, lp['norm_mix_g'], lp['w_in'], lp['segs'])
    r3 = lambda a: a.reshape(bsz, l, a.shape[-1])
    u_a, u_b, u_c, u_cz, u_dz, u_d, u_s = map(r3, (u_a, u_b, u_c, u_cz, u_dz, u_d, u_s))

    cache_pad = jnp.concatenate([jnp.zeros((bsz, CONV_A_PAD - (CONV_A - 1), wg), F32), buf_a.astype(F32)], axis=1)
    y_a, tail_a = _conformer(u_a, cache_pad, lp['conf_w'], lp['conf_b'], lp['conf_g'], lp['conf_beta'],
                             consts['mean_a'])
    new_buf_a = tail_a[:, CONV_A_PAD - (CONV_A - 1):, :]

    cos_t, sin_t = _rope_tables(pos, wg, dqk)
    q_b, kt_b, v_b, k_new, v_new = _attn_prep(u_b, cos_t, sin_t, lp['qg'], lp['kg'], consts['mean_qk'],
                                              consts['eye'], dqk)
    if k_past is not None:
        p = k_past.shape[1]
        ktp = _transpose_keys(k_past.reshape(bsz, p, wg).astype(F32), consts['eye'])
        vp4 = v_past.reshape(bsz, p, H_B, dv).astype(BF16)
        one = jnp.zeros((bsz, p, H_B, dv), BF16).at[..., 0].set(1.0)
        vp = jnp.concatenate([vp4, one], axis=-1).reshape(bsz, p, 2 * wg)
    else:
        ktp, vp = None, None
    y_b = _flash(q_b, kt_b, v_b, ktp, vp, lp['lam_p'], lp['subln_g'], lam_init, dqk, dv)

    hist_c = jnp.concatenate([jnp.zeros((bsz, SHORT_PAD - (CONV_SHORT - 1), 3 * wg), F32), buf_c.astype(F32)], axis=1)
    s0_bd = jnp.tile(s_c.astype(F32).reshape(bsz, wg, wg // H_C), (1, 1, H_C)) * consts['head_mask']
    y_c, s_bd = _gdn(u_c, u_cz, u_s, hist_c, s0_bd, lp['gdn_w'], lp['gdn_alog'], lp['gdn_dtb'], lp['gdn_og'],
                     consts['ones_c'], consts['tril'])
    dk = wg // H_C
    s_new = jnp.stack([s_bd[:, dk * h:dk * (h + 1), dk * h:dk * (h + 1)] for h in range(H_C)], axis=1)
    new_buf_c = u_c[:, l - (CONV_SHORT - 1):, :]

    wc = u_d.shape[-1]
    hist_d = jnp.concatenate([jnp.zeros((bsz, SHORT_PAD - (CONV_SHORT - 1), wc), F32), buf_d.astype(F32)], axis=1)
    h0 = h_d.astype(F32).reshape(bsz, wg, N_D)
    y_d, h_new = _ssd(u_d, u_dz, u_s, hist_d, h0, lp['ssd_w'], lp['ssd_b'], lp['ssd_alog'], lp['ssd_dtb'],
                      lp['ssd_dsk'], lp['ssd_ng'], consts['tril'])
    h_new = h_new.reshape(bsz, H_D, wg // H_D, N_D)
    new_buf_d = u_d[:, l - (CONV_SHORT - 1):, :]

    r2 = lambda a: a.reshape(bsz * l, wg)
    y2d = _out_proj_moe(x2d, [r2(y_a), r2(y_b), r2(y_c), r2(y_d)], lp['w_out'], lp['norm_ffn_g'], lp['moe_wr'],
                        lp['moe_br'], lp['moe_w13'], lp['moe_w2'])
    y = y2d.reshape(bsz, l, d)
    k_new = k_new.reshape(bsz, l, H_B, 2, dqk)
    v_new = v_new.reshape(bsz, l, H_B, dv)
    return y, new_buf_a, k_new, v_new, new_buf_c, s_new, new_buf_d, h_new


@jax.jit
def _forward(x_prompt, x_sample, cache_conv_conformer, cache_k_diff, cache_v_diff, cache_conv_delta,
             state_delta, cache_conv_ssd, state_ssd, params):
    bp, lp_, d = x_prompt.shape
    ls = x_sample.shape[1]
    depth = params['w_in'].shape[0]
    past_len = cache_k_diff.shape[2]
    wg = d // N_MIXERS
    dqk = wg // (2 * H_B)
    consts = dict(
        mean_a=_block_diag_const(wg, wg // A_GROUPS, 1.0 / (wg // A_GROUPS)),
        mean_qk=_block_diag_const(wg, dqk, 1.0 / dqk),
        ones_c=_block_diag_const(wg, wg // H_C, 1.0),
        eye=jnp.eye(wg, dtype=BF16),
        tril=jnp.asarray(np.tril(np.ones((CHUNK, CHUNK))), dtype=BF16),
        head_mask=jnp.asarray(np.kron(np.eye(H_C), np.ones((wg // H_C, wg // H_C))), dtype=F32),
    )
    pos_p = jnp.arange(lp_, dtype=jnp.int32)
    pos_s = past_len + jnp.arange(ls, dtype=jnp.int32)
    y_p, y_s = x_prompt, x_sample
    new_p = [[] for _ in range(7)]
    new_s = [[] for _ in range(7)]
    wc = wg + 2 * G_D * N_D
    for l in range(depth):
        lam_init = 0.8 - 0.6 * math.exp(-0.3 * l)
        lp = _layer_params(l, params, wg, d)
        y_p, *st_p = _trunk_layer(
            y_p, lp, consts, pos_p, lam_init,
            jnp.zeros((bp, CONV_A - 1, wg), F32), None, None,
            jnp.zeros((bp, CONV_SHORT - 1, 3 * wg), F32), jnp.zeros((bp, H_C, wg // H_C, wg // H_C), F32),
            jnp.zeros((bp, CONV_SHORT - 1, wc), F32), jnp.zeros((bp, H_D, wg // H_D, N_D), F32))
        y_s, *st_s = _trunk_layer(
            y_s, lp, consts, pos_s, lam_init,
            cache_conv_conformer[l], cache_k_diff[l], cache_v_diff[l],
            cache_conv_delta[l], state_delta[l], cache_conv_ssd[l], state_ssd[l])
        for i in range(7):
            new_p[i].append(st_p[i])
            new_s[i].append(st_s[i])
    outs_p = [jnp.stack(s, axis=0) for s in new_p]
    outs_s = [jnp.stack(s, axis=0) for s in new_s]
    return (y_p, y_s, *outs_p, *outs_s)


def kernel(x_prompt, x_sample, cache_conv_conformer, cache_k_diff, cache_v_diff, cache_conv_delta, state_delta, cache_conv_ssd, state_ssd, norm_mix_g, w_in, w_out, conf_conv_w, conf_conv_b, conf_ln_g, conf_ln_b, diff_qnorm_g, diff_knorm_g, diff_lambda, diff_subln_g, gdn_conv_w, gdn_A_log, gdn_dt_bias, gdn_onorm_g, ssd_conv_w, ssd_conv_b, ssd_A_log, ssd_dt_bias, ssd_D, ssd_norm_g, norm_ffn_g, moe_w_group, moe_b_group, moe_w_expert, moe_b_expert, moe_w1, moe_w3, moe_w2):
    params = dict(
        norm_mix_g=norm_mix_g, w_in=w_in, w_out=w_out, conf_conv_w=conf_conv_w, conf_conv_b=conf_conv_b,
        conf_ln_g=conf_ln_g, conf_ln_b=conf_ln_b, diff_qnorm_g=diff_qnorm_g, diff_knorm_g=diff_knorm_g,
        diff_lambda=diff_lambda, diff_subln_g=diff_subln_g, gdn_conv_w=gdn_conv_w, gdn_A_log=gdn_A_log,
        gdn_dt_bias=gdn_dt_bias, gdn_onorm_g=gdn_onorm_g, ssd_conv_w=ssd_conv_w, ssd_conv_b=ssd_conv_b,
        ssd_A_log=ssd_A_log, ssd_dt_bias=ssd_dt_bias, ssd_D=ssd_D, ssd_norm_g=ssd_norm_g,
        norm_ffn_g=norm_ffn_g, moe_w_group=moe_w_group, moe_b_group=moe_b_group, moe_w_expert=moe_w_expert,
        moe_b_expert=moe_b_expert, moe_w1=moe_w1, moe_w3=moe_w3, moe_w2=moe_w2)
    return _forward(x_prompt, x_sample, cache_conv_conformer, cache_k_diff, cache_v_diff, cache_conv_delta,
                    state_delta, cache_conv_ssd, state_ssd, params)
```

```python
import functools
import math

import numpy as np
import jax
import jax.numpy as jnp
from jax import lax
from jax.experimental import pallas as pl
from jax.experimental.pallas import tpu as pltpu

F32 = jnp.float32
BF16 = jnp.bfloat16

CHUNK = 64
N_MIXERS = 4
CONV_A = 31
A_GROUPS = 4
H_B = 4
ROPE_THETA = 500000.0
H_C = 4
CONV_SHORT = 4
H_D = 4
G_D = 2
N_D = 128
E_GROUPS = 4
E_PER_GROUP = 4
N_EXPERTS = E_GROUPS * E_PER_GROUP
EPS = 1e-6
NEG_BIG = -1e30

LANES = 128
SUBLANES = 8
CONV_A_PAD = 32
SHORT_PAD = 8
VMEM_LIMIT = 56 * 1024 * 1024
HEADS_PER_GROUP = 2


def _bdot(a, b):
    return jnp.dot(a.astype(BF16), b.astype(BF16), preferred_element_type=F32)


def _bdot_nt(a, b):
    return lax.dot_general(a.astype(BF16), b.astype(BF16), (((1,), (1,)), ((), ())),
                           preferred_element_type=F32)


def _dot3(a, b):
    a_hi = a.astype(BF16)
    a_lo = (a - a_hi.astype(F32)).astype(BF16)
    b_hi = b.astype(BF16)
    b_lo = (b - b_hi.astype(F32)).astype(BF16)
    return (jnp.dot(a_hi, b_hi, preferred_element_type=F32) + jnp.dot(a_hi, b_lo, preferred_element_type=F32)
            + jnp.dot(a_lo, b_hi, preferred_element_type=F32))


def _split_dot(x, m, n_split=3):
    acc = None
    r = x
    for i in range(n_split):
        p = r.astype(BF16)
        d = jnp.dot(p, m, preferred_element_type=F32)
        acc = d if acc is None else acc + d
        if i + 1 < n_split:
            r = r - p.astype(F32)
    return acc


def _split_dot_rhs(m, x, n_split=3):
    acc = None
    r = x
    for i in range(n_split):
        p = r.astype(BF16)
        d = jnp.dot(m, p, preferred_element_type=F32)
        acc = d if acc is None else acc + d
        if i + 1 < n_split:
            r = r - p.astype(F32)
    return acc


def _sigmoid(x):
    return 1.0 / (1.0 + jnp.exp(-x))


def _silu(x):
    return x * _sigmoid(x)


def _softplus(x):
    return jnp.maximum(x, 0.0) + jnp.log1p(jnp.exp(-jnp.abs(x)))


def _tile4(a):
    return jnp.concatenate([a, a, a, a], axis=0)


def _fold4(a, n):
    return a[0:n] + a[n:2 * n] + a[2 * n:3 * n] + a[3 * n:4 * n]


def _cparams(sem):
    return pltpu.CompilerParams(dimension_semantics=sem, vmem_limit_bytes=VMEM_LIMIT)


def _const_spec(shape):
    nd = len(shape)
    return pl.BlockSpec(shape, lambda *_: (0,) * nd)


def _in_proj_kernel(x_ref, g_ref, w_ref, *out_refs, segs):
    x = x_ref[...]
    ms = jnp.mean(x * x, axis=-1, keepdims=True)
    xn = (x * lax.rsqrt(ms + EPS) * g_ref[...]).astype(BF16)
    off = 0
    for o_ref, n in zip(out_refs, segs):
        o_ref[...] = jnp.dot(xn, w_ref[:, off:off + n], preferred_element_type=F32)
        off += n


def _in_proj(x2d, g, w_pad, segs):
    t, d = x2d.shape
    tm = min(512, t)
    n_all = sum(segs)
    return pl.pallas_call(
        functools.partial(_in_proj_kernel, segs=segs),
        out_shape=[jax.ShapeDtypeStruct((t, n), F32) for n in segs],
        grid=(t // tm,),
        in_specs=[pl.BlockSpec((tm, d), lambda i: (i, 0)),
                  _const_spec((1, d)),
                  _const_spec((d, n_all))],
        out_specs=[pl.BlockSpec((tm, n), lambda i: (i, 0)) for n in segs],
        compiler_params=_cparams(("parallel",)),
        name="in_proj",
    )(x2d, g, w_pad)


def _conf_kernel(u_ref, cache_ref, w_ref, b_ref, g_ref, beta_ref, pm_ref, y_ref, tail_ref, abuf, shifted,
                 *, tl, wg):
    t = pl.program_id(1)

    @pl.when(t == 0)
    def _():
        abuf[0:CONV_A_PAD, :] = cache_ref[0]

    u = u_ref[0]
    a = u[:, :wg] * _sigmoid(u[:, wg:])
    abuf[CONV_A_PAD:CONV_A_PAD + tl, :] = a
    span = tl + CONV_A_PAD - SUBLANES
    for r in range(1, SUBLANES):
        shifted[r - 1] = abuf[r:r + span, :]
    acc = jnp.zeros((tl, wg), F32) + b_ref[...]
    base = CONV_A_PAD - (CONV_A - 1)
    for j in range(CONV_A):
        q, r = divmod(base + j, SUBLANES)
        win = abuf[SUBLANES * q:SUBLANES * q + tl, :] if r == 0 else shifted[r - 1, SUBLANES * q:SUBLANES * q + tl, :]
        acc = acc + w_ref[j:j + 1, :] * win
    pm = pm_ref[...]
    mu = _split_dot(acc, pm)
    dlt = acc - mu
    var = _split_dot(dlt * dlt, pm)
    cn = dlt * lax.rsqrt(var + EPS) * g_ref[...] + beta_ref[...]
    y_ref[0] = _silu(cn).astype(BF16)
    tail = abuf[tl:tl + CONV_A_PAD, :]
    tail_ref[0] = tail
    abuf[0:CONV_A_PAD, :] = tail


def _conformer(u_a, cache_pad, w_pad, b, g, beta, pm):
    bsz, l, two_wg = u_a.shape
    wg = two_wg // 2
    tl = min(512, l)
    return pl.pallas_call(
        functools.partial(_conf_kernel, tl=tl, wg=wg),
        out_shape=[jax.ShapeDtypeStruct((bsz, l, wg), BF16),
                   jax.ShapeDtypeStruct((bsz, CONV_A_PAD, wg), F32)],
        grid=(bsz, l // tl),
        in_specs=[pl.BlockSpec((1, tl, two_wg), lambda b_, t_: (b_, t_, 0)),
                  pl.BlockSpec((1, CONV_A_PAD, wg), lambda b_, t_: (b_, 0, 0)),
                  _const_spec((CONV_A_PAD, wg)),
                  _const_spec((1, wg)), _const_spec((1, wg)), _const_spec((1, wg)),
                  _const_spec((wg, wg))],
        out_specs=[pl.BlockSpec((1, tl, wg), lambda b_, t_: (b_, t_, 0)),
                   pl.BlockSpec((1, CONV_A_PAD, wg), lambda b_, t_: (b_, 0, 0))],
        scratch_shapes=[pltpu.VMEM((CONV_A_PAD + tl, wg), F32),
                        pltpu.VMEM((SUBLANES - 1, tl + CONV_A_PAD - SUBLANES, wg), F32)],
        compiler_params=_cparams(("parallel", "arbitrary")),
        name="conformer",
    )(u_a, cache_pad, w_pad, b, g, beta, pm)


def _attn_prep_kernel(*refs, wg, dqk, n_alias):
    qkv_ref, cos_ref, sin_ref, qg_ref, kg_ref, pm_ref, eye_ref = refs[:7]
    q_ref, kt_ref, vb_ref, kn_ref, vn_ref = refs[7 + n_alias:]
    qkv = qkv_ref[0]
    cos = cos_ref[...]
    sin = sin_ref[...]
    pm = pm_ref[...]
    lane = lax.broadcasted_iota(jnp.int32, (1, wg), 1)
    low = (lane % dqk) < (dqk // 8)

    def norm_rope(x, g):
        ms = _split_dot(x * x, pm)
        xn = x * lax.rsqrt(ms + EPS) * g
        rot = jnp.where(low, pltpu.roll(xn, wg - dqk // 8, 1), pltpu.roll(xn, dqk // 8, 1))
        return xn * cos + rot * sin

    q = norm_rope(qkv[:, :wg], qg_ref[...])
    k = norm_rope(qkv[:, wg:2 * wg], kg_ref[...])
    v = qkv[:, 2 * wg:]
    q_ref[0] = q.astype(BF16)
    kn_ref[0, 0] = k
    vn_ref[0, 0] = v
    dv = 2 * dqk
    pad = jnp.where(lax.broadcasted_iota(jnp.int32, (v.shape[0], dv), 1) == 0, 1.0, 0.0)
    vb_ref[0] = jnp.concatenate(
        [blk for h in range(H_B) for blk in (v[:, dv * h:dv * (h + 1)], pad)], axis=1).astype(BF16)
    kt_ref[0] = _bdot_nt(eye_ref[...], k).astype(BF16)


def _attn_prep(qkv, cos_t, sin_t, qg, kg, pm, eye, dqk, layer, depth, kv_acc):
    bsz, l, w3 = qkv.shape
    wg = w3 // 3
    tl = min(512, l)
    n_alias = 0 if kv_acc is None else 2
    kern = functools.partial(_attn_prep_kernel, wg=wg, dqk=dqk, n_alias=n_alias)
    n_in = 7
    layered = pl.BlockSpec((1, 1, tl, wg), lambda t_, b_: (layer, b_, t_, 0))
    return pl.pallas_call(
        kern,
        out_shape=[jax.ShapeDtypeStruct((bsz, l, wg), BF16),
                   jax.ShapeDtypeStruct((bsz, wg, l), BF16),
                   jax.ShapeDtypeStruct((bsz, l, 2 * wg), BF16),
                   jax.ShapeDtypeStruct((depth, bsz, l, wg), F32),
                   jax.ShapeDtypeStruct((depth, bsz, l, wg), F32)],
        grid=(l // tl, bsz),
        in_specs=[pl.BlockSpec((1, tl, w3), lambda t_, b_: (b_, t_, 0)),
                  pl.BlockSpec((tl, wg), lambda t_, b_: (t_, 0)),
                  pl.BlockSpec((tl, wg), lambda t_, b_: (t_, 0)),
                  _const_spec((1, wg)), _const_spec((1, wg)),
                  _const_spec((wg, wg)), _const_spec((wg, wg))] + [pl.BlockSpec(memory_space=pl.ANY)] * n_alias,
        out_specs=[pl.BlockSpec((1, tl, wg), lambda t_, b_: (b_, t_, 0)),
                   pl.BlockSpec((1, wg, tl), lambda t_, b_: (b_, 0, t_)),
                   pl.BlockSpec((1, tl, 2 * wg), lambda t_, b_: (b_, t_, 0)),
                   layered, layered],
        input_output_aliases={n_in: 3, n_in + 1: 4} if n_alias else {},
        compiler_params=_cparams(("parallel", "parallel")),
        name="attn_prep",
    )(qkv, cos_t, sin_t, qg, kg, pm, eye, *(kv_acc or ()))


def _kt_kernel(k_ref, eye_ref, kt_ref):
    kt_ref[0] = _bdot_nt(eye_ref[...], k_ref[0]).astype(BF16)


def _transpose_keys(k, eye):
    bsz, p, w = k.shape
    tp = min(512, p)
    return pl.pallas_call(
        _kt_kernel,
        out_shape=jax.ShapeDtypeStruct((bsz, w, p), BF16),
        grid=(bsz, p // tp),
        in_specs=[pl.BlockSpec((1, tp, w), lambda b_, t_: (b_, t_, 0)), _const_spec((w, w))],
        out_specs=pl.BlockSpec((1, w, tp), lambda b_, t_: (b_, 0, t_)),
        compiler_params=_cparams(("parallel", "parallel")),
        name="transpose_keys",
    )(k, eye)


def _flash_kernel(*refs, tq, tkb, tkp, n_past, nq, dqk, dv, lam_init):
    if n_past:
        q_ref, kt_ref, v_ref, ktp_ref, vp_ref, lam_ref, sg_ref, o_ref = refs
    else:
        q_ref, kt_ref, v_ref, lam_ref, sg_ref, o_ref = refs
    i = pl.program_id(1)
    lp = lam_ref[...]
    lam = (jnp.exp(jnp.sum(lp[0:1] * lp[1:2], axis=-1, keepdims=True))
           - jnp.exp(jnp.sum(lp[2:3] * lp[3:4], axis=-1, keepdims=True)) + lam_init)
    sg = sg_ref[...]
    row_c = lax.broadcasted_iota(jnp.int32, (tq, tq), 0) // CHUNK
    col_c = lax.broadcasted_iota(jnp.int32, (tq, tq), 1) // CHUNK
    diag_mask = col_c <= row_c
    hw = 2 * dqk
    vw = 2 * dv
    c2 = (dqk ** -0.5) * math.log2(math.e)

    def update(qm, ktm, vt, carry, mask):
        m, acc = carry
        s = jnp.dot(qm, ktm, preferred_element_type=F32)
        if mask is not None:
            s = jnp.where(mask, s, NEG_BIG)
        m_new = jnp.maximum(m, jnp.max(s, axis=-1, keepdims=True))
        alpha = jnp.exp2((m - m_new) * c2)
        p = jnp.exp2((s - m_new) * c2)
        acc = alpha * acc + jnp.dot(p.astype(BF16), vt, preferred_element_type=F32)
        return m_new, acc

    heads = range(H_B)
    rows = [slice(hw * h, hw * (h + 1)) for h in heads]
    cols = [slice(vw * h, vw * (h + 1)) for h in heads]
    qs = []
    for h in heads:
        qh = q_ref[0, :, rows[h]]
        qs.append((qh[:, :dqk], qh[:, dqk:]))

    def finish(o):
        ms = jnp.mean(o * o, axis=-1, keepdims=True)
        return o * lax.rsqrt(ms + EPS) * sg * (1.0 - lam_init)

    if nq == 1:
        outs = []
        for h in heads:
            kts = [ktp_ref[0, rows[h], :]] if n_past else []
            vts = [vp_ref[0, :, cols[h]]] if n_past else []
            kt = jnp.concatenate(kts + [kt_ref[0, rows[h], :]], axis=1)
            vt = jnp.concatenate(vts + [v_ref[0, :, cols[h]]], axis=0)
            n_old = kt.shape[1] - tq
            vis = jnp.concatenate([jnp.full((tq, n_old), True), diag_mask], axis=1) if n_old else diag_mask

            def probs(qm, ktm):
                s = jnp.dot(qm, ktm, preferred_element_type=F32) * (dqk ** -0.5)
                s = jnp.where(vis, s, NEG_BIG)
                e = jnp.exp(s - jnp.max(s, axis=-1, keepdims=True))
                return e / jnp.sum(e, axis=-1, keepdims=True)

            pd = probs(qs[h][0], kt[:dqk]) - lam * probs(qs[h][1], kt[dqk:])
            outs.append(finish(jnp.dot(pd.astype(BF16), vt, preferred_element_type=F32)[:, :dv]))
        o_ref[0] = jnp.concatenate(outs, axis=1).astype(BF16)
        return

    def tile_group(carries, group, kt_of, vt_of, mask):
        chains = [(h, c) for h in group for c in range(2)]
        kts = {h: kt_of(rows[h]) for h in group}
        vts = {h: vt_of(cols[h]) for h in group}
        s = [jnp.dot(qs[h][c], kts[h][dqk * c:dqk * (c + 1)], preferred_element_type=F32) for h, c in chains]
        if mask is not None:
            s = [jnp.where(mask, x, NEG_BIG) for x in s]
        m_old = [carries[h][c][0] for h, c in chains]
        m_new = [jnp.maximum(m, jnp.max(x, axis=-1, keepdims=True)) for m, x in zip(m_old, s)]
        p = [jnp.exp2((x - m) * c2).astype(BF16) for x, m in zip(s, m_new)]
        pv = [jnp.dot(x, vts[h], preferred_element_type=F32) for x, (h, c) in zip(p, chains)]
        acc = [jnp.exp2((mo - mn) * c2) * carries[h][c][1] + y
               for mo, mn, y, (h, c) in zip(m_old, m_new, pv, chains)]
        return [((m_new[2 * g], acc[2 * g]), (m_new[2 * g + 1], acc[2 * g + 1])) for g in range(len(group))]

    def tile(carries, kt_of, vt_of, mask):
        out = []
        for g0 in range(0, H_B, HEADS_PER_GROUP):
            out += tile_group(carries, list(range(g0, g0 + HEADS_PER_GROUP)), kt_of, vt_of, mask)
        return tuple(out)

    init = (jnp.full((tq, 1), NEG_BIG, F32), jnp.zeros((tq, vw), F32))
    carries = tuple((init, init) for _ in heads)
    for j in range(n_past):
        carries = tile(carries, lambda r: ktp_ref[0, r, j * tkp:(j + 1) * tkp],
                       lambda c: vp_ref[0, j * tkp:(j + 1) * tkp, c], None)

    start = i * tq
    n_big = start // tkb if tkb > tq else 0

    def keys(off, width):
        return (lambda r: kt_ref[0, r, pl.ds(off, width)]), (lambda c: v_ref[0, pl.ds(off, width), c])

    def body_big(j, carries):
        return tile(carries, *keys(pl.multiple_of(j * tkb, tkb), tkb), None)

    n_small = (start - n_big * tkb) // tq

    def body_small(j, carries):
        bump = jnp.where(j == n_small, 0, tq // CHUNK)
        return tile(carries, *keys(pl.multiple_of(n_big * tkb + j * tq, tq), tq), col_c <= row_c + bump)

    if tkb > tq:
        carries = lax.fori_loop(0, n_big, body_big, carries)
    carries = lax.fori_loop(0, n_small + 1, body_small, carries)

    outs = []
    for h in heads:
        (_, a0), (_, a1) = carries[h]
        outs.append(finish(a0[:, :dv] * (1.0 / a0[:, dv:dv + 1]) - lam * (a1[:, :dv] * (1.0 / a1[:, dv:dv + 1]))))
    o_ref[0] = jnp.concatenate(outs, axis=1).astype(BF16)


def _flash(q, kt, v1, ktp, vp1, lam_p, sg, lam_init, dqk, dv):
    bsz, l, wg = q.shape
    vw_all = v1.shape[2]
    tq = min(512, l)
    tkb = min(1024, l)
    nq = l // tq
    n_past = 0
    tkp = 0
    args = [q, kt, v1]
    in_specs = [pl.BlockSpec((1, tq, wg), lambda b_, i_: (b_, i_, 0)),
                pl.BlockSpec((1, wg, l), lambda b_, i_: (b_, 0, 0)),
                pl.BlockSpec((1, l, vw_all), lambda b_, i_: (b_, 0, 0))]
    if ktp is not None:
        p = ktp.shape[2]
        tkp = min(2048, p)
        n_past = p // tkp
        args += [ktp, vp1]
        in_specs += [pl.BlockSpec((1, wg, p), lambda b_, i_: (b_, 0, 0)),
                     pl.BlockSpec((1, p, vw_all), lambda b_, i_: (b_, 0, 0))]
    args += [lam_p, sg]
    in_specs += [_const_spec(lam_p.shape), _const_spec(sg.shape)]
    kern = functools.partial(_flash_kernel, tq=tq, tkb=tkb, tkp=tkp, n_past=n_past, nq=nq, dqk=dqk, dv=dv,
                             lam_init=lam_init)
    return pl.pallas_call(
        kern,
        out_shape=jax.ShapeDtypeStruct((bsz, l, wg), BF16),
        grid=(bsz, nq),
        in_specs=in_specs,
        out_specs=pl.BlockSpec((1, tq, wg), lambda b_, i_: (b_, i_, 0)),
        compiler_params=_cparams(("parallel", "parallel")),
        name="flash_diff_attn",
    )(*args)


def _short_conv(x_ref, hist_ref, w_ref, xbuf, t, tl):
    @pl.when(t == 0)
    def _():
        xbuf[0:SHORT_PAD, :] = hist_ref[0]

    xbuf[SHORT_PAD:SHORT_PAD + tl, :] = x_ref[0]
    base = SHORT_PAD - (CONV_SHORT - 1)
    acc = w_ref[0:1, :] * xbuf[base:base + tl, :]
    for j in range(1, CONV_SHORT):
        acc = acc + w_ref[j:j + 1, :] * xbuf[base + j:base + j + tl, :]
    xbuf[0:SHORT_PAD, :] = xbuf[tl:tl + SHORT_PAD, :]
    return acc


def _bd_masks(n):
    r = lax.broadcasted_iota(jnp.int32, (n, n), 0)
    c = lax.broadcasted_iota(jnp.int32, (n, n), 1)
    head = (r // CHUNK) == (c // CHUNK)
    incl = head & (c <= r)
    strict = head & (c < r)
    return head, incl, strict, r == c


def _expand_col(x4, lane0, n):
    r = lax.broadcasted_iota(jnp.int32, (n, LANES), 0) // CHUNK
    ln = lax.broadcasted_iota(jnp.int32, (n, LANES), 1)
    return jnp.sum(jnp.where(ln == r + lane0, _tile4(x4), 0.0), axis=-1, keepdims=True)


def _expand_row(x1, lane0, n):
    r = lax.broadcasted_iota(jnp.int32, (n, LANES), 0) // CHUNK
    ln = lax.broadcasted_iota(jnp.int32, (n, LANES), 1)
    return jnp.sum(jnp.where(ln == r + lane0, jnp.broadcast_to(x1, (n, LANES)), 0.0), axis=-1, keepdims=True)


def _decay_matrix(cum_r, incl, n):
    cm = jnp.broadcast_to(cum_r, (n, n))
    return jnp.where(incl, jnp.exp(jnp.minimum(cm - cm.T, 0.0)), 0.0)


def _gdn_kernel(qkv_ref, z_ref, sm_ref, hist_ref, s0_ref, cw_ref, alog_ref, dtb_ref, og_ref, ones_ref, tril_ref,
                y_ref, sout_ref, xbuf, s_scr, q_s, k_s, v_s, b_s, g_s, o_s, *, tl, wg):
    t = pl.program_id(1)
    n = wg

    @pl.when(t == 0)
    def _():
        s_scr[...] = s0_ref[0]

    c = _silu(_short_conv(qkv_ref, hist_ref, cw_ref, xbuf, t, tl))
    ones_bd = ones_ref[...]
    q = c[:, :wg]
    k = c[:, wg:2 * wg]
    dk = wg // H_C
    q_s[...] = q * lax.rsqrt(_split_dot(q * q, ones_bd) + EPS) * (dk ** -0.5)
    k_s[...] = k * lax.rsqrt(_split_dot(k * k, ones_bd) + EPS)
    v_s[...] = c[:, 2 * wg:]
    sm = sm_ref[0]
    b_s[...] = _sigmoid(sm)
    g_s[...] = -jnp.exp(alog_ref[...]) * _softplus(sm + dtb_ref[...])

    head, incl, strict, eye = _bd_masks(n)
    eye_f = jnp.where(eye, 1.0, 0.0)
    tril = tril_ref[...]

    sls = [slice(ci * CHUNK, (ci + 1) * CHUNK) for ci in range(tl // CHUNK)]
    cum = [_split_dot_rhs(tril, g_s[sl, :]) for sl in sls]
    beta_r = [_expand_col(b_s[sl, :], 0, n) for sl in sls]
    cum_r = [_expand_col(cm, H_C, n) for cm in cum]
    tot_r = [_expand_row(cm[CHUNK - 1:CHUNK, :], H_C, n) for cm in cum]
    kx = [jnp.where(head, _tile4(k_s[sl, :]), 0.0) for sl in sls]
    qx = [jnp.where(head, _tile4(q_s[sl, :]), 0.0) for sl in sls]
    vx = [jnp.where(head, _tile4(v_s[sl, :]), 0.0) for sl in sls]
    dm = [_decay_matrix(cr, incl, n) for cr in cum_r]
    a = [jnp.where(strict, br * _bdot_nt(kc, kc) * dc, 0.0) for br, kc, dc in zip(beta_r, kx, dm)]
    x = [eye_f - ac for ac in a]
    p = a
    for _ in range(int(math.log2(CHUNK)) - 2):
        p = [_bdot(pc, pc) for pc in p]
        x = [xc + _bdot(xc, pc) for xc, pc in zip(x, p)]
    x = [xc + _bdot(xc, eye_f - xc - _dot3(ac, xc)) for xc, ac in zip(x, a)]
    ecum = [jnp.exp(cr) for cr in cum_r]
    u = [_dot3(xc, br * vc) for xc, br, vc in zip(x, beta_r, vx)]
    w = [_dot3(xc, (br * ec) * kc) for xc, br, ec, kc in zip(x, beta_r, ecum, kx)]
    qe = [qc * ec for qc, ec in zip(qx, ecum)]
    qkd = [_bdot_nt(qc, kc) * dc for qc, kc, dc in zip(qx, kx, dm)]
    kdt = [(kc * jnp.exp(tr - cr)).T for kc, tr, cr in zip(kx, tot_r, cum_r)]
    s = s_scr[...]
    for ci, sl in enumerate(sls):
        vn = u[ci] - _bdot(w[ci], s)
        o = _bdot(qe[ci], s) + _bdot(qkd[ci], vn)
        o_s[sl, :] = _fold4(o, CHUNK)
        s = s * jnp.exp(tot_r[ci]) + _bdot(kdt[ci], vn)
    s_scr[...] = s
    o = o_s[...]
    ms = _split_dot(o * o, ones_bd) * (1.0 / dk)
    y_ref[0] = (o * lax.rsqrt(ms + EPS) * og_ref[...] * _silu(z_ref[0])).astype(BF16)
    sout_ref[0] = s_scr[...]


def _gdn(qkv, z, small, hist, s0_bd, cw, alog, dtb, og, ones_bd, tril):
    bsz, l, w3 = qkv.shape
    wg = w3 // 3
    tl = min(256, l)
    kern = functools.partial(_gdn_kernel, tl=tl, wg=wg)
    bt = lambda b_, t_: (b_, t_, 0)
    b0 = lambda b_, t_: (b_, 0, 0)
    return pl.pallas_call(
        kern,
        out_shape=[jax.ShapeDtypeStruct((bsz, l, wg), BF16),
                   jax.ShapeDtypeStruct((bsz, wg, wg), F32)],
        grid=(bsz, l // tl),
        in_specs=[pl.BlockSpec((1, tl, w3), bt),
                  pl.BlockSpec((1, tl, wg), bt),
                  pl.BlockSpec((1, tl, LANES), bt),
                  pl.BlockSpec((1, SHORT_PAD, w3), b0),
                  pl.BlockSpec((1, wg, wg), b0),
                  _const_spec((SHORT_PAD, w3)),
                  _const_spec((1, LANES)), _const_spec((1, LANES)), _const_spec((1, wg)),
                  _const_spec((wg, wg)), _const_spec((CHUNK, CHUNK))],
        out_specs=[pl.BlockSpec((1, tl, wg), bt),
                   pl.BlockSpec((1, wg, wg), b0)],
        scratch_shapes=[pltpu.VMEM((SHORT_PAD + tl, w3), F32),
                        pltpu.VMEM((wg, wg), F32),
                        pltpu.VMEM((tl, wg), F32), pltpu.VMEM((tl, wg), F32), pltpu.VMEM((tl, wg), F32),
                        pltpu.VMEM((tl, LANES), F32), pltpu.VMEM((tl, LANES), F32),
                        pltpu.VMEM((tl, wg), F32)],
        compiler_params=_cparams(("parallel", "arbitrary")),
        name="gated_deltanet",
    )(qkv, z, small, hist, s0_bd, cw, alog, dtb, og, ones_bd, tril)


def _ssd_kernel(xbc_ref, z_ref, sm_ref, hist_ref, h0_ref, cw_ref, cb_ref, alog_ref, dtb_ref, dsk_ref, ng_ref,
                tril_ref, y_ref, hout_ref, xbuf, h_scr, x_s, b_s, c_s, dt_s, da_s, y_s, *, tl, wg):
    t = pl.program_id(1)
    n = wg

    @pl.when(t == 0)
    def _():
        h_scr[...] = h0_ref[0]

    c = _silu(_short_conv(xbc_ref, hist_ref, cw_ref, xbuf, t, tl) + cb_ref[...])
    xs = c[:, :wg]
    x_s[...] = xs
    b_s[...] = c[:, wg:wg + G_D * N_D]
    c_s[...] = c[:, wg + G_D * N_D:]
    dt = _softplus(sm_ref[0] + dtb_ref[...])
    dt_s[...] = dt
    da_s[...] = -jnp.exp(alog_ref[...]) * dt

    head, incl, _, _ = _bd_masks(n)
    tril = tril_ref[...]
    lane0 = H_C + H_C

    rep = H_D // G_D

    def per_head(a):
        return jnp.concatenate([a[:, N_D * (hh // rep):N_D * (hh // rep + 1)] for hh in range(H_D)], axis=0)

    sls = [slice(ci * CHUNK, (ci + 1) * CHUNK) for ci in range(tl // CHUNK)]
    acs = [_split_dot_rhs(tril, da_s[sl, :]) for sl in sls]
    dt_r = [_expand_col(dt_s[sl, :], lane0, n) for sl in sls]
    acs_r = [_expand_col(ac, lane0, n) for ac in acs]
    tot_r = [_expand_row(ac[CHUNK - 1:CHUNK, :], lane0, n) for ac in acs]
    xdt = [jnp.where(head, _tile4(x_s[sl, :]), 0.0) * dr for sl, dr in zip(sls, dt_r)]
    cn = [per_head(c_s[sl, :]) for sl in sls]
    bn = [per_head(b_s[sl, :]) for sl in sls]
    scores = [_bdot_nt(cc, bc) * _decay_matrix(ar, incl, n) for cc, bc, ar in zip(cn, bn, acs_r)]
    y_diag = [_bdot(sc, xc) for sc, xc in zip(scores, xdt)]
    s_chunk = [_bdot(xc.T, bc * jnp.exp(tr - ar)) for xc, bc, tr, ar in zip(xdt, bn, tot_r, acs_r)]
    c_dec = [cc * jnp.exp(ar) for cc, ar in zip(cn, acs_r)]
    hs = h_scr[...]
    for ci, sl in enumerate(sls):
        y_bd = y_diag[ci] + jnp.where(head, _bdot_nt(c_dec[ci], hs), 0.0)
        y_s[sl, :] = _fold4(y_bd, CHUNK)
        hs = hs * jnp.exp(tot_r[ci]) + s_chunk[ci]
    h_scr[...] = hs
    y = (y_s[...] + dsk_ref[...] * xs) * _silu(z_ref[0])
    gw = wg // G_D
    parts = []
    for g in range(G_D):
        yg = y[:, gw * g:gw * (g + 1)]
        ms = jnp.mean(yg * yg, axis=-1, keepdims=True)
        parts.append(yg * lax.rsqrt(ms + EPS))
    y_ref[0] = (jnp.concatenate(parts, axis=1) * ng_ref[...]).astype(BF16)
    hout_ref[0] = h_scr[...]


def _ssd(xbc, z, small, hist, h0, cw, cb, alog, dtb, dsk, ng, tril):
    bsz, l, wc = xbc.shape
    wg = z.shape[2]
    tl = min(256, l)
    kern = functools.partial(_ssd_kernel, tl=tl, wg=wg)
    bt = lambda b_, t_: (b_, t_, 0)
    b0 = lambda b_, t_: (b_, 0, 0)
    return pl.pallas_call(
        kern,
        out_shape=[jax.ShapeDtypeStruct((bsz, l, wg), BF16),
                   jax.ShapeDtypeStruct((bsz, wg, N_D), F32)],
        grid=(bsz, l // tl),
        in_specs=[pl.BlockSpec((1, tl, wc), bt),
                  pl.BlockSpec((1, tl, wg), bt),
                  pl.BlockSpec((1, tl, LANES), bt),
                  pl.BlockSpec((1, SHORT_PAD, wc), b0),
                  pl.BlockSpec((1, wg, N_D), b0),
                  _const_spec((SHORT_PAD, wc)), _const_spec((1, wc)),
                  _const_spec((1, LANES)), _const_spec((1, LANES)), _const_spec((1, wg)), _const_spec((1, wg)),
                  _const_spec((CHUNK, CHUNK))],
        out_specs=[pl.BlockSpec((1, tl, wg), bt),
                   pl.BlockSpec((1, wg, N_D), b0)],
        scratch_shapes=[pltpu.VMEM((SHORT_PAD + tl, wc), F32),
                        pltpu.VMEM((wg, N_D), F32),
                        pltpu.VMEM((tl, wg), F32), pltpu.VMEM((tl, G_D * N_D), F32), pltpu.VMEM((tl, G_D * N_D), F32),
                        pltpu.VMEM((tl, LANES), F32), pltpu.VMEM((tl, LANES), F32),
                        pltpu.VMEM((tl, wg), F32)],
        compiler_params=_cparams(("parallel", "arbitrary")),
        name="ssd",
    )(xbc, z, small, hist, h0, cw, cb, alog, dtb, dsk, ng, tril)


def _route(logits):
    lane = lax.broadcasted_iota(jnp.int32, logits.shape, 1).astype(F32)
    big = float(LANES)
    lg = jnp.where(lane < E_GROUPS, logits, NEG_BIG)
    mg = jnp.max(lg, axis=-1, keepdims=True)
    sg = jnp.sum(jnp.exp(lg - mg), axis=-1, keepdims=True)
    gidx = jnp.min(jnp.where(lg == mg, lane, big), axis=-1, keepdims=True)
    w_grp = 1.0 / sg
    lo = E_GROUPS + E_PER_GROUP * gidx
    sel = (lane >= lo) & (lane < lo + E_PER_GROUP)
    le = jnp.where(sel, logits, NEG_BIG)
    me = jnp.max(le, axis=-1, keepdims=True)
    pe = jnp.where(sel, jnp.exp(le - me), 0.0)
    p_in = pe / jnp.sum(pe, axis=-1, keepdims=True)
    v1 = jnp.max(p_in, axis=-1, keepdims=True)
    i1 = jnp.min(jnp.where(sel & (p_in == v1), lane, big), axis=-1, keepdims=True)
    rest = sel & (lane != i1)
    p2 = jnp.where(rest, p_in, -1.0)
    v2 = jnp.max(p2, axis=-1, keepdims=True)
    i2 = jnp.min(jnp.where(rest & (p2 == v2), lane, big), axis=-1, keepdims=True)
    den = v1 + v2
    gate = jnp.where(lane == i1, v1 / den, 0.0) + jnp.where(lane == i2, v2 / den, 0.0)
    return gate * w_grp


def _moe_kernel(x_ref, ya_ref, yb_ref, yc_ref, yd_ref, wo_ref, g_ref, wr_ref, br_ref, w13_ref, w2_ref, o_ref, act_s,
                *, f, ne, wg):
    h = x_ref[...]
    for i, y_ref in enumerate((ya_ref, yb_ref, yc_ref, yd_ref)):
        h = h + jnp.dot(y_ref[...], wo_ref[wg * i:wg * (i + 1), :], preferred_element_type=F32)
    ms = jnp.mean(h * h, axis=-1, keepdims=True)
    xn = (h * lax.rsqrt(ms + EPS) * g_ref[...]).astype(BF16)
    gate = _route(jnp.dot(xn, wr_ref[...], preferred_element_type=F32) + br_ref[...])
    for e in range(ne):
        hid = jnp.dot(xn, w13_ref[e], preferred_element_type=F32)
        act = _silu(hid[:, :f]) * hid[:, f:]
        act_s[:, f * e:f * (e + 1)] = (act * gate[:, E_GROUPS + e:E_GROUPS + e + 1]).astype(BF16)
    o_ref[...] = h + jnp.dot(act_s[...], w2_ref[...], preferred_element_type=F32)


def _out_proj_moe(x2d, ys, w_out, g, wr, br, w13, w2s):
    t, d = x2d.shape
    wg = ys[0].shape[1]
    ne, _, f2 = w13.shape
    f = f2 // 2
    tm = min(512, t)
    row = lambda i: (i, 0)
    resident = dict(pipeline_mode=pl.Buffered(1))
    return pl.pallas_call(
        functools.partial(_moe_kernel, f=f, ne=ne, wg=wg),
        out_shape=jax.ShapeDtypeStruct((t, d), F32),
        grid=(t // tm,),
        in_specs=[pl.BlockSpec((tm, d), row)] + [pl.BlockSpec((tm, wg), row)] * 4 + [
                  pl.BlockSpec(w_out.shape, lambda i: (0, 0), **resident),
                  _const_spec((1, d)), _const_spec((d, LANES)), _const_spec((1, LANES)),
                  pl.BlockSpec((ne, d, f2), lambda i: (0, 0, 0), **resident),
                  pl.BlockSpec((ne * f, d), lambda i: (0, 0), **resident)],
        out_specs=pl.BlockSpec((tm, d), row),
        scratch_shapes=[pltpu.VMEM((tm, ne * f), BF16)],
        compiler_params=_cparams(("parallel",)),
        name="out_proj_moe",
    )(x2d, *ys, w_out, g, wr, br, w13, w2s)


def _block_diag_const(n, blk, val):
    r = np.arange(n)
    return jnp.asarray(np.where((r[:, None] // blk) == (r[None, :] // blk), val, 0.0), dtype=BF16)


def _rope_tables(pos, wg, dqk):
    rot = dqk // 4
    half = rot // 2
    inv = jnp.exp(jnp.arange(half, dtype=F32) * (-2.0 / rot) * math.log(ROPE_THETA))
    ang = pos.astype(F32)[:, None] * inv[None, :]
    cos, sin = jnp.cos(ang), jnp.sin(ang)
    l = pos.shape[0]
    ones = jnp.ones((l, dqk - rot), F32)
    zeros = jnp.zeros((l, dqk - rot), F32)
    cos_d = jnp.concatenate([cos, cos, ones], axis=1)
    sin_d = jnp.concatenate([-sin, sin, zeros], axis=1)
    reps = wg // dqk
    return jnp.tile(cos_d, (1, reps)), jnp.tile(sin_d, (1, reps))


def _pad_lanes(v, lane0):
    out = jnp.zeros((1, LANES), F32)
    return out.at[0, lane0:lane0 + v.shape[0]].set(v.astype(F32))


def _layer_params(l, p, wg, d):
    sizes = (2 * wg, 3 * wg, 3 * wg, wg, H_C, H_C, wg, wg + 2 * G_D * N_D, H_D)
    cuts = np.cumsum((0,) + sizes)
    w_in = p['w_in'][l]
    seg = lambda i: w_in[:, cuts[i]:cuts[i + 1]]
    n_small = H_C + H_C + H_D
    w_pad = jnp.concatenate([seg(0), seg(1), seg(2), seg(3), seg(6), seg(7), seg(4), seg(5), seg(8),
                             jnp.zeros((d, LANES - n_small), F32)], axis=1).astype(BF16)
    segs = (sizes[0], sizes[1], sizes[2], sizes[3], sizes[6], sizes[7], LANES)
    dqk = wg // (2 * H_B)
    row = lambda v: v.astype(F32).reshape(1, -1)
    wc = sizes[7]
    lp = dict(
        segs=segs, w_in=w_pad, norm_mix_g=row(p['norm_mix_g'][l]),
        w_out=p['w_out'][l].astype(BF16),
        conf_w=jnp.concatenate([p['conf_conv_w'][l], jnp.zeros((CONV_A_PAD - CONV_A, wg), F32)], axis=0),
        conf_b=row(p['conf_conv_b'][l]), conf_g=row(p['conf_ln_g'][l]), conf_beta=row(p['conf_ln_b'][l]),
        qg=row(jnp.tile(p['diff_qnorm_g'][l], wg // dqk)), kg=row(jnp.tile(p['diff_knorm_g'][l], wg // dqk)),
        lam_p=p['diff_lambda'][l].astype(F32), subln_g=row(p['diff_subln_g'][l]),
        gdn_w=jnp.concatenate([p['gdn_conv_w'][l], jnp.zeros((SHORT_PAD - CONV_SHORT, 3 * wg), F32)], axis=0),
        gdn_alog=_pad_lanes(p['gdn_A_log'][l], H_C), gdn_dtb=_pad_lanes(p['gdn_dt_bias'][l], H_C),
        gdn_og=row(jnp.tile(p['gdn_onorm_g'][l], H_C)),
        ssd_w=jnp.concatenate([p['ssd_conv_w'][l], jnp.zeros((SHORT_PAD - CONV_SHORT, wc), F32)], axis=0),
        ssd_b=row(p['ssd_conv_b'][l]),
        ssd_alog=_pad_lanes(p['ssd_A_log'][l], 2 * H_C), ssd_dtb=_pad_lanes(p['ssd_dt_bias'][l], 2 * H_C),
        ssd_dsk=row(jnp.repeat(p['ssd_D'][l], wg // H_D)), ssd_ng=row(p['ssd_norm_g'][l]),
        norm_ffn_g=row(p['norm_ffn_g'][l]),
        moe_wr=jnp.concatenate([p['moe_w_group'][l], p['moe_w_expert'][l],
                                jnp.zeros((d, LANES - E_GROUPS - N_EXPERTS), F32)], axis=1).astype(BF16),
        moe_br=jnp.concatenate([p['moe_b_group'][l], p['moe_b_expert'][l],
                                jnp.zeros((LANES - E_GROUPS - N_EXPERTS,), F32)]).reshape(1, LANES),
        moe_w13=jnp.concatenate([p['moe_w1'][l], p['moe_w3'][l]], axis=-1).astype(BF16),
        moe_w2=p['moe_w2'][l].astype(BF16).reshape(-1, d),
    )
    return lp


def _trunk_layer(x, lp, consts, pos, lam_init, layer, depth, kv_acc, buf_a, k_past, v_past, buf_c, s_c, buf_d, h_d):
    bsz, l, d = x.shape
    wg = d // N_MIXERS
    dqk = wg // (2 * H_B)
    dv = wg // H_B
    assert wg // H_C == CHUNK and wg // H_D == CHUNK and H_C == H_D == N_MIXERS
    assert l % CHUNK == 0 and l >= CONV_A_PAD
    x2d = x.reshape(bsz * l, d)
    u_a, u_b, u_c, u_cz, u_dz, u_d, u_s = _in_proj(x2d, lp['norm_mix_g'], lp['w_in'], lp['segs'])
    r3 = lambda a: a.reshape(bsz, l, a.shape[-1])
    u_a, u_b, u_c, u_cz, u_dz, u_d, u_s = map(r3, (u_a, u_b, u_c, u_cz, u_dz, u_d, u_s))

    cache_pad = jnp.concatenate([jnp.zeros((bsz, CONV_A_PAD - (CONV_A - 1), wg), F32), buf_a.astype(F32)], axis=1)
    y_a, tail_a = _conformer(u_a, cache_pad, lp['conf_w'], lp['conf_b'], lp['conf_g'], lp['conf_beta'],
                             consts['mean_a'])
    new_buf_a = tail_a[:, CONV_A_PAD - (CONV_A - 1):, :]

    cos_t, sin_t = _rope_tables(pos, wg, dqk)
    q_b, kt_b, v_b, k_all, v_all = _attn_prep(u_b, cos_t, sin_t, lp['qg'], lp['kg'], consts['mean_qk'],
                                              consts['eye'], dqk, layer, depth, kv_acc)
    if k_past is not None:
        p = k_past.shape[1]
        ktp = _transpose_keys(k_past.reshape(bsz, p, wg).astype(F32), consts['eye'])
        vp4 = v_past.reshape(bsz, p, H_B, dv).astype(BF16)
        one = jnp.zeros((bsz, p, H_B, dv), BF16).at[..., 0].set(1.0)
        vp = jnp.concatenate([vp4, one], axis=-1).reshape(bsz, p, 2 * wg)
    else:
        ktp, vp = None, None
    y_b = _flash(q_b, kt_b, v_b, ktp, vp, lp['lam_p'], lp['subln_g'], lam_init, dqk, dv)

    hist_c = jnp.concatenate([jnp.zeros((bsz, SHORT_PAD - (CONV_SHORT - 1), 3 * wg), F32), buf_c.astype(F32)], axis=1)
    s0_bd = jnp.tile(s_c.astype(F32).reshape(bsz, wg, wg // H_C), (1, 1, H_C)) * consts['head_mask']
    y_c, s_bd = _gdn(u_c, u_cz, u_s, hist_c, s0_bd, lp['gdn_w'], lp['gdn_alog'], lp['gdn_dtb'], lp['gdn_og'],
                     consts['ones_c'], consts['tril'])
    dk = wg // H_C
    s_new = jnp.stack([s_bd[:, dk * h:dk * (h + 1), dk * h:dk * (h + 1)] for h in range(H_C)], axis=1)
    new_buf_c = u_c[:, l - (CONV_SHORT - 1):, :]

    wc = u_d.shape[-1]
    hist_d = jnp.concatenate([jnp.zeros((bsz, SHORT_PAD - (CONV_SHORT - 1), wc), F32), buf_d.astype(F32)], axis=1)
    h0 = h_d.astype(F32).reshape(bsz, wg, N_D)
    y_d, h_new = _ssd(u_d, u_dz, u_s, hist_d, h0, lp['ssd_w'], lp['ssd_b'], lp['ssd_alog'], lp['ssd_dtb'],
                      lp['ssd_dsk'], lp['ssd_ng'], consts['tril'])
    h_new = h_new.reshape(bsz, H_D, wg // H_D, N_D)
    new_buf_d = u_d[:, l - (CONV_SHORT - 1):, :]

    r2 = lambda a: a.reshape(bsz * l, wg)
    y2d = _out_proj_moe(x2d, [r2(y_a), r2(y_b), r2(y_c), r2(y_d)], lp['w_out'], lp['norm_ffn_g'], lp['moe_wr'],
                        lp['moe_br'], lp['moe_w13'], lp['moe_w2'])
    y = y2d.reshape(bsz, l, d)
    return y, new_buf_a, (k_all, v_all), new_buf_c, s_new, new_buf_d, h_new


@jax.jit
def _forward(x_prompt, x_sample, cache_conv_conformer, cache_k_diff, cache_v_diff, cache_conv_delta,
             state_delta, cache_conv_ssd, state_ssd, params):
    bp, lp_, d = x_prompt.shape
    ls = x_sample.shape[1]
    depth = params['w_in'].shape[0]
    past_len = cache_k_diff.shape[2]
    wg = d // N_MIXERS
    dqk = wg // (2 * H_B)
    consts = dict(
        mean_a=_block_diag_const(wg, wg // A_GROUPS, 1.0 / (wg // A_GROUPS)),
        mean_qk=_block_diag_const(wg, dqk, 1.0 / dqk),
        ones_c=_block_diag_const(wg, wg // H_C, 1.0),
        eye=jnp.eye(wg, dtype=BF16),
        tril=jnp.asarray(np.tril(np.ones((CHUNK, CHUNK))), dtype=BF16),
        head_mask=jnp.asarray(np.kron(np.eye(H_C), np.ones((wg // H_C, wg // H_C))), dtype=F32),
    )
    pos_p = jnp.arange(lp_, dtype=jnp.int32)
    pos_s = past_len + jnp.arange(ls, dtype=jnp.int32)
    y_p, y_s = x_prompt, x_sample
    new_p = [[] for _ in range(5)]
    new_s = [[] for _ in range(5)]
    kv_p, kv_s = None, None
    wc = wg + 2 * G_D * N_D
    for l in range(depth):
        lam_init = 0.8 - 0.6 * math.exp(-0.3 * l)
        lp = _layer_params(l, params, wg, d)
        y_p, buf_a, kv_p, *st_p = _trunk_layer(
            y_p, lp, consts, pos_p, lam_init, l, depth, kv_p,
            jnp.zeros((bp, CONV_A - 1, wg), F32), None, None,
            jnp.zeros((bp, CONV_SHORT - 1, 3 * wg), F32), jnp.zeros((bp, H_C, wg // H_C, wg // H_C), F32),
            jnp.zeros((bp, CONV_SHORT - 1, wc), F32), jnp.zeros((bp, H_D, wg // H_D, N_D), F32))
        st_p = [buf_a] + st_p
        y_s, buf_a, kv_s, *st_s = _trunk_layer(
            y_s, lp, consts, pos_s, lam_init, l, depth, kv_s,
            cache_conv_conformer[l], cache_k_diff[l], cache_v_diff[l],
            cache_conv_delta[l], state_delta[l], cache_conv_ssd[l], state_ssd[l])
        st_s = [buf_a] + st_s
        for i in range(5):
            new_p[i].append(st_p[i])
            new_s[i].append(st_s[i])

    def assemble(stacked, kv, bsz, l):
        conv_a, conv_c, s_c, conv_d, h_d = [jnp.stack(s, axis=0) for s in stacked]
        k_all = kv[0].reshape(depth, bsz, l, H_B, 2, dqk)
        v_all = kv[1].reshape(depth, bsz, l, H_B, wg // H_B)
        return conv_a, k_all, v_all, conv_c, s_c, conv_d, h_d

    return (y_p, y_s, *assemble(new_p, kv_p, bp, lp_), *assemble(new_s, kv_s, x_sample.shape[0], ls))


def kernel(x_prompt, x_sample, cache_conv_conformer, cache_k_diff, cache_v_diff, cache_conv_delta, state_delta, cache_conv_ssd, state_ssd, norm_mix_g, w_in, w_out, conf_conv_w, conf_conv_b, conf_ln_g, conf_ln_b, diff_qnorm_g, diff_knorm_g, diff_lambda, diff_subln_g, gdn_conv_w, gdn_A_log, gdn_dt_bias, gdn_onorm_g, ssd_conv_w, ssd_conv_b, ssd_A_log, ssd_dt_bias, ssd_D, ssd_norm_g, norm_ffn_g, moe_w_group, moe_b_group, moe_w_expert, moe_b_expert, moe_w1, moe_w3, moe_w2):
    params = dict(
        norm_mix_g=norm_mix_g, w_in=w_in, w_out=w_out, conf_conv_w=conf_conv_w, conf_conv_b=conf_conv_b,
        conf_ln_g=conf_ln_g, conf_ln_b=conf_ln_b, diff_qnorm_g=diff_qnorm_g, diff_knorm_g=diff_knorm_g,
        diff_lambda=diff_lambda, diff_subln_g=diff_subln_g, gdn_conv_w=gdn_conv_w, gdn_A_log=gdn_A_log,
        gdn_dt_bias=gdn_dt_bias, gdn_onorm_g=gdn_onorm_g, ssd_conv_w=ssd_conv_w, ssd_conv_b=ssd_conv_b,
        ssd_A_log=ssd_A_log, ssd_dt_bias=ssd_dt_bias, ssd_D=ssd_D, ssd_norm_g=ssd_norm_g,
        norm_ffn_g=norm_ffn_g, moe_w_group=moe_w_group, moe_b_group=moe_b_group, moe_w_expert=moe_w_expert,
        moe_b_expert=moe_b_expert, moe_w1=moe_w1, moe_w3=moe_w3, moe_w2=moe_w2)
    return _forward(x_prompt, x_sample, cache_conv_conformer, cache_k_diff, cache_v_diff, cache_conv_delta,
                    state_delta, cache_conv_ssd, state_ssd, params)
```

```python
import functools
import math

import numpy as np
import jax
import jax.numpy as jnp
from jax import lax
from jax.experimental import pallas as pl
from jax.experimental.pallas import tpu as pltpu

F32 = jnp.float32
BF16 = jnp.bfloat16

CHUNK = 64
N_MIXERS = 4
CONV_A = 31
A_GROUPS = 4
H_B = 4
ROPE_THETA = 500000.0
H_C = 4
CONV_SHORT = 4
H_D = 4
G_D = 2
N_D = 128
E_GROUPS = 4
E_PER_GROUP = 4
N_EXPERTS = E_GROUPS * E_PER_GROUP
EPS = 1e-6
NEG_BIG = -1e30

LANES = 128
SUBLANES = 8
CONV_A_PAD = 32
SHORT_PAD = 8
VMEM_LIMIT = 56 * 1024 * 1024
HEADS_PER_GROUP = 2


def _bdot(a, b):
    return jnp.dot(a.astype(BF16), b.astype(BF16), preferred_element_type=F32)


def _bdot_nt(a, b):
    return lax.dot_general(a.astype(BF16), b.astype(BF16), (((1,), (1,)), ((), ())),
                           preferred_element_type=F32)


def _dot3(a, b):
    a_hi = a.astype(BF16)
    a_lo = (a - a_hi.astype(F32)).astype(BF16)
    b_hi = b.astype(BF16)
    b_lo = (b - b_hi.astype(F32)).astype(BF16)
    return (jnp.dot(a_hi, b_hi, preferred_element_type=F32) + jnp.dot(a_hi, b_lo, preferred_element_type=F32)
            + jnp.dot(a_lo, b_hi, preferred_element_type=F32))


def _split_dot(x, m, n_split=3):
    acc = None
    r = x
    for i in range(n_split):
        p = r.astype(BF16)
        d = jnp.dot(p, m, preferred_element_type=F32)
        acc = d if acc is None else acc + d
        if i + 1 < n_split:
            r = r - p.astype(F32)
    return acc


def _split_dot_rhs(m, x, n_split=3):
    acc = None
    r = x
    for i in range(n_split):
        p = r.astype(BF16)
        d = jnp.dot(m, p, preferred_element_type=F32)
        acc = d if acc is None else acc + d
        if i + 1 < n_split:
            r = r - p.astype(F32)
    return acc


def _sigmoid(x):
    return 1.0 / (1.0 + jnp.exp(-x))


def _silu(x):
    return x * _sigmoid(x)


def _softplus(x):
    return jnp.maximum(x, 0.0) + jnp.log1p(jnp.exp(-jnp.abs(x)))


def _tile4(a):
    return jnp.concatenate([a, a, a, a], axis=0)


def _fold4(a, n):
    return a[0:n] + a[n:2 * n] + a[2 * n:3 * n] + a[3 * n:4 * n]


def _cparams(sem):
    return pltpu.CompilerParams(dimension_semantics=sem, vmem_limit_bytes=VMEM_LIMIT)


def _const_spec(shape):
    nd = len(shape)
    return pl.BlockSpec(shape, lambda *_: (0,) * nd)


def _in_proj_kernel(x_ref, g_ref, w_ref, *out_refs, segs):
    x = x_ref[...]
    ms = jnp.mean(x * x, axis=-1, keepdims=True)
    xn = (x * lax.rsqrt(ms + EPS) * g_ref[...]).astype(BF16)
    off = 0
    for o_ref, n in zip(out_refs, segs):
        o_ref[...] = jnp.dot(xn, w_ref[:, off:off + n], preferred_element_type=F32)
        off += n


def _in_proj(x2d, g, w_pad, segs):
    t, d = x2d.shape
    tm = min(512, t)
    n_all = sum(segs)
    return pl.pallas_call(
        functools.partial(_in_proj_kernel, segs=segs),
        out_shape=[jax.ShapeDtypeStruct((t, n), F32) for n in segs],
        grid=(t // tm,),
        in_specs=[pl.BlockSpec((tm, d), lambda i: (i, 0)),
                  _const_spec((1, d)),
                  _const_spec((d, n_all))],
        out_specs=[pl.BlockSpec((tm, n), lambda i: (i, 0)) for n in segs],
        compiler_params=_cparams(("parallel",)),
        name="in_proj",
    )(x2d, g, w_pad)


def _conf_kernel(u_ref, cache_ref, w_ref, b_ref, g_ref, beta_ref, pm_ref, y_ref, tail_ref, abuf, shifted,
                 *, tl, wg):
    t = pl.program_id(1)

    @pl.when(t == 0)
    def _():
        abuf[0:CONV_A_PAD, :] = cache_ref[0]

    u = u_ref[0]
    a = u[:, :wg] * _sigmoid(u[:, wg:])
    abuf[CONV_A_PAD:CONV_A_PAD + tl, :] = a
    span = tl + CONV_A_PAD - SUBLANES
    for r in range(1, SUBLANES):
        shifted[r - 1] = abuf[r:r + span, :]
    acc = jnp.zeros((tl, wg), F32) + b_ref[...]
    base = CONV_A_PAD - (CONV_A - 1)
    for j in range(CONV_A):
        q, r = divmod(base + j, SUBLANES)
        win = abuf[SUBLANES * q:SUBLANES * q + tl, :] if r == 0 else shifted[r - 1, SUBLANES * q:SUBLANES * q + tl, :]
        acc = acc + w_ref[j:j + 1, :] * win
    pm = pm_ref[...]
    mu = _split_dot(acc, pm)
    dlt = acc - mu
    var = _split_dot(dlt * dlt, pm)
    cn = dlt * lax.rsqrt(var + EPS) * g_ref[...] + beta_ref[...]
    y_ref[0] = _silu(cn).astype(BF16)
    tail = abuf[tl:tl + CONV_A_PAD, :]
    tail_ref[0] = tail
    abuf[0:CONV_A_PAD, :] = tail


def _conformer(u_a, cache_pad, w_pad, b, g, beta, pm):
    bsz, l, two_wg = u_a.shape
    wg = two_wg // 2
    tl = min(512, l)
    return pl.pallas_call(
        functools.partial(_conf_kernel, tl=tl, wg=wg),
        out_shape=[jax.ShapeDtypeStruct((bsz, l, wg), BF16),
                   jax.ShapeDtypeStruct((bsz, CONV_A_PAD, wg), F32)],
        grid=(bsz, l // tl),
        in_specs=[pl.BlockSpec((1, tl, two_wg), lambda b_, t_: (b_, t_, 0)),
                  pl.BlockSpec((1, CONV_A_PAD, wg), lambda b_, t_: (b_, 0, 0)),
                  _const_spec((CONV_A_PAD, wg)),
                  _const_spec((1, wg)), _const_spec((1, wg)), _const_spec((1, wg)),
                  _const_spec((wg, wg))],
        out_specs=[pl.BlockSpec((1, tl, wg), lambda b_, t_: (b_, t_, 0)),
                   pl.BlockSpec((1, CONV_A_PAD, wg), lambda b_, t_: (b_, 0, 0))],
        scratch_shapes=[pltpu.VMEM((CONV_A_PAD + tl, wg), F32),
                        pltpu.VMEM((SUBLANES - 1, tl + CONV_A_PAD - SUBLANES, wg), F32)],
        compiler_params=_cparams(("parallel", "arbitrary")),
        name="conformer",
    )(u_a, cache_pad, w_pad, b, g, beta, pm)


def _attn_prep_kernel(qkv_ref, cos_ref, sin_ref, qg_ref, kg_ref, pm_ref, eye_ref,
                      q_ref, kt_ref, vb_ref, kn_ref, vn_ref, *, wg, dqk):
    qkv = qkv_ref[0]
    cos = cos_ref[...]
    sin = sin_ref[...]
    pm = pm_ref[...]
    lane = lax.broadcasted_iota(jnp.int32, (1, wg), 1)
    low = (lane % dqk) < (dqk // 8)

    def norm_rope(x, g):
        ms = _split_dot(x * x, pm)
        xn = x * lax.rsqrt(ms + EPS) * g
        rot = jnp.where(low, pltpu.roll(xn, wg - dqk // 8, 1), pltpu.roll(xn, dqk // 8, 1))
        return xn * cos + rot * sin

    q = norm_rope(qkv[:, :wg], qg_ref[...])
    k = norm_rope(qkv[:, wg:2 * wg], kg_ref[...])
    v = qkv[:, 2 * wg:]
    q_ref[0] = q.astype(BF16)
    kn_ref[0] = k
    vn_ref[0] = v
    dv = 2 * dqk
    pad = jnp.where(lax.broadcasted_iota(jnp.int32, (v.shape[0], dv), 1) == 0, 1.0, 0.0)
    vb_ref[0] = jnp.concatenate(
        [blk for h in range(H_B) for blk in (v[:, dv * h:dv * (h + 1)], pad)], axis=1).astype(BF16)
    kt_ref[0] = _bdot_nt(eye_ref[...], k).astype(BF16)


def _attn_prep(qkv, cos_t, sin_t, qg, kg, pm, eye, dqk):
    bsz, l, w3 = qkv.shape
    wg = w3 // 3
    tl = min(512, l)
    kern = functools.partial(_attn_prep_kernel, wg=wg, dqk=dqk)
    return pl.pallas_call(
        kern,
        out_shape=[jax.ShapeDtypeStruct((bsz, l, wg), BF16),
                   jax.ShapeDtypeStruct((bsz, wg, l), BF16),
                   jax.ShapeDtypeStruct((bsz, l, 2 * wg), BF16),
                   jax.ShapeDtypeStruct((bsz, l, wg), F32),
                   jax.ShapeDtypeStruct((bsz, l, wg), F32)],
        grid=(l // tl, bsz),
        in_specs=[pl.BlockSpec((1, tl, w3), lambda t_, b_: (b_, t_, 0)),
                  pl.BlockSpec((tl, wg), lambda t_, b_: (t_, 0)),
                  pl.BlockSpec((tl, wg), lambda t_, b_: (t_, 0)),
                  _const_spec((1, wg)), _const_spec((1, wg)),
                  _const_spec((wg, wg)), _const_spec((wg, wg))],
        out_specs=[pl.BlockSpec((1, tl, wg), lambda t_, b_: (b_, t_, 0)),
                   pl.BlockSpec((1, wg, tl), lambda t_, b_: (b_, 0, t_)),
                   pl.BlockSpec((1, tl, 2 * wg), lambda t_, b_: (b_, t_, 0)),
                   pl.BlockSpec((1, tl, wg), lambda t_, b_: (b_, t_, 0)),
                   pl.BlockSpec((1, tl, wg), lambda t_, b_: (b_, t_, 0))],
        compiler_params=_cparams(("parallel", "parallel")),
        name="attn_prep",
    )(qkv, cos_t, sin_t, qg, kg, pm, eye)


def _kt_kernel(k_ref, eye_ref, kt_ref):
    kt_ref[0] = _bdot_nt(eye_ref[...], k_ref[0]).astype(BF16)


def _transpose_keys(k, eye):
    bsz, p, w = k.shape
    tp = min(512, p)
    return pl.pallas_call(
        _kt_kernel,
        out_shape=jax.ShapeDtypeStruct((bsz, w, p), BF16),
        grid=(bsz, p // tp),
        in_specs=[pl.BlockSpec((1, tp, w), lambda b_, t_: (b_, t_, 0)), _const_spec((w, w))],
        out_specs=pl.BlockSpec((1, w, tp), lambda b_, t_: (b_, 0, t_)),
        compiler_params=_cparams(("parallel", "parallel")),
        name="transpose_keys",
    )(k, eye)


def _flash_kernel(*refs, tq, tkb, tkp, n_past, nq, dqk, dv, lam_init):
    if n_past:
        q_ref, kt_ref, v_ref, ktp_ref, vp_ref, lam_ref, sg_ref, o_ref = refs
    else:
        q_ref, kt_ref, v_ref, lam_ref, sg_ref, o_ref = refs
    i = pl.program_id(1)
    lp = lam_ref[...]
    lam = (jnp.exp(jnp.sum(lp[0:1] * lp[1:2], axis=-1, keepdims=True))
           - jnp.exp(jnp.sum(lp[2:3] * lp[3:4], axis=-1, keepdims=True)) + lam_init)
    sg = sg_ref[...]
    row_c = lax.broadcasted_iota(jnp.int32, (tq, tq), 0) // CHUNK
    col_c = lax.broadcasted_iota(jnp.int32, (tq, tq), 1) // CHUNK
    diag_mask = col_c <= row_c
    hw = 2 * dqk
    vw = 2 * dv
    c2 = (dqk ** -0.5) * math.log2(math.e)

    def update(qm, ktm, vt, carry, mask):
        m, acc = carry
        s = jnp.dot(qm, ktm, preferred_element_type=F32)
        if mask is not None:
            s = jnp.where(mask, s, NEG_BIG)
        m_new = jnp.maximum(m, jnp.max(s, axis=-1, keepdims=True))
        alpha = jnp.exp2((m - m_new) * c2)
        p = jnp.exp2((s - m_new) * c2)
        acc = alpha * acc + jnp.dot(p.astype(BF16), vt, preferred_element_type=F32)
        return m_new, acc

    heads = range(H_B)
    rows = [slice(hw * h, hw * (h + 1)) for h in heads]
    cols = [slice(vw * h, vw * (h + 1)) for h in heads]
    qs = []
    for h in heads:
        qh = q_ref[0, :, rows[h]]
        qs.append((qh[:, :dqk], qh[:, dqk:]))

    def finish(o):
        ms = jnp.mean(o * o, axis=-1, keepdims=True)
        return o * lax.rsqrt(ms + EPS) * sg * (1.0 - lam_init)

    if nq == 1:
        outs = []
        for h in heads:
            kts = [ktp_ref[0, rows[h], :]] if n_past else []
            vts = [vp_ref[0, :, cols[h]]] if n_past else []
            kt = jnp.concatenate(kts + [kt_ref[0, rows[h], :]], axis=1)
            vt = jnp.concatenate(vts + [v_ref[0, :, cols[h]]], axis=0)
            n_old = kt.shape[1] - tq
            vis = jnp.concatenate([jnp.full((tq, n_old), True), diag_mask], axis=1) if n_old else diag_mask

            def probs(qm, ktm):
                s = jnp.dot(qm, ktm, preferred_element_type=F32) * (dqk ** -0.5)
                s = jnp.where(vis, s, NEG_BIG)
                e = jnp.exp(s - jnp.max(s, axis=-1, keepdims=True))
                return e / jnp.sum(e, axis=-1, keepdims=True)

            pd = probs(qs[h][0], kt[:dqk]) - lam * probs(qs[h][1], kt[dqk:])
            outs.append(finish(jnp.dot(pd.astype(BF16), vt, preferred_element_type=F32)[:, :dv]))
        o_ref[0] = jnp.concatenate(outs, axis=1).astype(BF16)
        return

    def tile_group(carries, group, kt_of, vt_of, mask):
        chains = [(h, c) for h in group for c in range(2)]
        kts = {h: kt_of(rows[h]) for h in group}
        vts = {h: vt_of(cols[h]) for h in group}
        s = [jnp.dot(qs[h][c], kts[h][dqk * c:dqk * (c + 1)], preferred_element_type=F32) for h, c in chains]
        if mask is not None:
            s = [jnp.where(mask, x, NEG_BIG) for x in s]
        m_old = [carries[h][c][0] for h, c in chains]
        m_new = [jnp.maximum(m, jnp.max(x, axis=-1, keepdims=True)) for m, x in zip(m_old, s)]
        p = [jnp.exp2((x - m) * c2).astype(BF16) for x, m in zip(s, m_new)]
        pv = [jnp.dot(x, vts[h], preferred_element_type=F32) for x, (h, c) in zip(p, chains)]
        acc = [jnp.exp2((mo - mn) * c2) * carries[h][c][1] + y
               for mo, mn, y, (h, c) in zip(m_old, m_new, pv, chains)]
        return [((m_new[2 * g], acc[2 * g]), (m_new[2 * g + 1], acc[2 * g + 1])) for g in range(len(group))]

    def tile(carries, kt_of, vt_of, mask):
        out = []
        for g0 in range(0, H_B, HEADS_PER_GROUP):
            out += tile_group(carries, list(range(g0, g0 + HEADS_PER_GROUP)), kt_of, vt_of, mask)
        return tuple(out)

    init = (jnp.full((tq, 1), NEG_BIG, F32), jnp.zeros((tq, vw), F32))
    carries = tuple((init, init) for _ in heads)
    for j in range(n_past):
        carries = tile(carries, lambda r: ktp_ref[0, r, j * tkp:(j + 1) * tkp],
                       lambda c: vp_ref[0, j * tkp:(j + 1) * tkp, c], None)

    start = i * tq
    n_big = start // tkb if tkb > tq else 0

    def keys(off, width):
        return (lambda r: kt_ref[0, r, pl.ds(off, width)]), (lambda c: v_ref[0, pl.ds(off, width), c])

    def body_big(j, carries):
        return tile(carries, *keys(pl.multiple_of(j * tkb, tkb), tkb), None)

    n_small = (start - n_big * tkb) // tq

    def body_small(j, carries):
        bump = jnp.where(j == n_small, 0, tq // CHUNK)
        return tile(carries, *keys(pl.multiple_of(n_big * tkb + j * tq, tq), tq), col_c <= row_c + bump)

    if tkb > tq:
        carries = lax.fori_loop(0, n_big, body_big, carries)
    carries = lax.fori_loop(0, n_small + 1, body_small, carries)

    outs = []
    for h in heads:
        (_, a0), (_, a1) = carries[h]
        outs.append(finish(a0[:, :dv] * (1.0 / a0[:, dv:dv + 1]) - lam * (a1[:, :dv] * (1.0 / a1[:, dv:dv + 1]))))
    o_ref[0] = jnp.concatenate(outs, axis=1).astype(BF16)


def _flash(q, kt, v1, ktp, vp1, lam_p, sg, lam_init, dqk, dv):
    bsz, l, wg = q.shape
    vw_all = v1.shape[2]
    tq = min(512, l)
    tkb = min(1024, l)
    nq = l // tq
    n_past = 0
    tkp = 0
    args = [q, kt, v1]
    in_specs = [pl.BlockSpec((1, tq, wg), lambda b_, i_: (b_, i_, 0)),
                pl.BlockSpec((1, wg, l), lambda b_, i_: (b_, 0, 0)),
                pl.BlockSpec((1, l, vw_all), lambda b_, i_: (b_, 0, 0))]
    if ktp is not None:
        p = ktp.shape[2]
        tkp = min(2048, p)
        n_past = p // tkp
        args += [ktp, vp1]
        in_specs += [pl.BlockSpec((1, wg, p), lambda b_, i_: (b_, 0, 0)),
                     pl.BlockSpec((1, p, vw_all), lambda b_, i_: (b_, 0, 0))]
    args += [lam_p, sg]
    in_specs += [_const_spec(lam_p.shape), _const_spec(sg.shape)]
    kern = functools.partial(_flash_kernel, tq=tq, tkb=tkb, tkp=tkp, n_past=n_past, nq=nq, dqk=dqk, dv=dv,
                             lam_init=lam_init)
    return pl.pallas_call(
        kern,
        out_shape=jax.ShapeDtypeStruct((bsz, l, wg), BF16),
        grid=(bsz, nq),
        in_specs=in_specs,
        out_specs=pl.BlockSpec((1, tq, wg), lambda b_, i_: (b_, i_, 0)),
        compiler_params=_cparams(("parallel", "parallel")),
        name="flash_diff_attn",
    )(*args)


def _short_conv(x_ref, hist_ref, w_ref, xbuf, t, tl):
    @pl.when(t == 0)
    def _():
        xbuf[0:SHORT_PAD, :] = hist_ref[0]

    xbuf[SHORT_PAD:SHORT_PAD + tl, :] = x_ref[0]
    base = SHORT_PAD - (CONV_SHORT - 1)
    acc = w_ref[0:1, :] * xbuf[base:base + tl, :]
    for j in range(1, CONV_SHORT):
        acc = acc + w_ref[j:j + 1, :] * xbuf[base + j:base + j + tl, :]
    xbuf[0:SHORT_PAD, :] = xbuf[tl:tl + SHORT_PAD, :]
    return acc


def _bd_mask_consts(n):
    r = np.arange(n)
    head = (r[:, None] // CHUNK) == (r[None, :] // CHUNK)
    incl = head & (r[None, :] <= r[:, None])
    strict = head & (r[None, :] < r[:, None])
    return jnp.asarray(np.stack([head, incl, strict, np.eye(n, dtype=bool)]), dtype=F32)


def _sel_consts(n, lane0s):
    r = np.arange(n)[:, None] // CHUNK
    ln = np.arange(LANES)[None, :]
    return jnp.asarray(np.stack([ln == r + l0 for l0 in lane0s]), dtype=F32)


def _expand_col(x4, sel):
    return jnp.sum(_tile4(x4) * sel, axis=-1, keepdims=True)


def _expand_row(x1, sel):
    return jnp.sum(jnp.broadcast_to(x1, sel.shape) * sel, axis=-1, keepdims=True)


def _decay_matrix(cum_r, incl, n):
    cm = jnp.broadcast_to(cum_r, (n, n))
    return jnp.exp(jnp.minimum(cm - cm.T, 0.0)) * incl


def _gdn_kernel(qkv_ref, z_ref, sm_ref, hist_ref, s0_ref, cw_ref, alog_ref, dtb_ref, og_ref, ones_ref, tril_ref,
                mask_ref, sel_ref, y_ref, sout_ref, xbuf, s_scr, q_s, k_s, v_s, b_s, g_s, o_s, *, tl, wg):
    t = pl.program_id(1)
    n = wg

    @pl.when(t == 0)
    def _():
        s_scr[...] = s0_ref[0]

    c = _silu(_short_conv(qkv_ref, hist_ref, cw_ref, xbuf, t, tl))
    ones_bd = ones_ref[...]
    q = c[:, :wg]
    k = c[:, wg:2 * wg]
    dk = wg // H_C
    q_s[...] = q * lax.rsqrt(_split_dot(q * q, ones_bd) + EPS) * (dk ** -0.5)
    k_s[...] = k * lax.rsqrt(_split_dot(k * k, ones_bd) + EPS)
    v_s[...] = c[:, 2 * wg:]
    sm = sm_ref[0]
    b_s[...] = _sigmoid(sm)
    g_s[...] = -jnp.exp(alog_ref[...]) * _softplus(sm + dtb_ref[...])

    head, incl, strict, eye_f = mask_ref[0], mask_ref[1], mask_ref[2], mask_ref[3]
    sel_b, sel_g = sel_ref[0], sel_ref[1]
    tril = tril_ref[...]

    sls = [slice(ci * CHUNK, (ci + 1) * CHUNK) for ci in range(tl // CHUNK)]
    cum = [_split_dot_rhs(tril, g_s[sl, :]) for sl in sls]
    beta_r = [_expand_col(b_s[sl, :], sel_b) for sl in sls]
    cum_r = [_expand_col(cm, sel_g) for cm in cum]
    tot_r = [_expand_row(cm[CHUNK - 1:CHUNK, :], sel_g) for cm in cum]
    kx = [_tile4(k_s[sl, :]) * head for sl in sls]
    qx = [_tile4(q_s[sl, :]) * head for sl in sls]
    vx = [_tile4(v_s[sl, :]) * head for sl in sls]
    dm = [_decay_matrix(cr, incl, n) for cr in cum_r]
    a = [(br * _bdot_nt(kc, kc) * dc) * strict for br, kc, dc in zip(beta_r, kx, dm)]
    x = [eye_f - ac for ac in a]
    p = a
    for _ in range(int(math.log2(CHUNK)) - 2):
        p = [_bdot(pc, pc) for pc in p]
        x = [xc + _bdot(xc, pc) for xc, pc in zip(x, p)]
    x = [xc + _bdot(xc, eye_f - xc - _dot3(ac, xc)) for xc, ac in zip(x, a)]
    ecum = [jnp.exp(cr) for cr in cum_r]
    u = [_dot3(xc, br * vc) for xc, br, vc in zip(x, beta_r, vx)]
    w = [_dot3(xc, (br * ec) * kc) for xc, br, ec, kc in zip(x, beta_r, ecum, kx)]
    qe = [qc * ec for qc, ec in zip(qx, ecum)]
    qkd = [_bdot_nt(qc, kc) * dc for qc, kc, dc in zip(qx, kx, dm)]
    kdt = [(kc * jnp.exp(tr - cr)).T for kc, tr, cr in zip(kx, tot_r, cum_r)]
    s = s_scr[...]
    for ci, sl in enumerate(sls):
        vn = u[ci] - _bdot(w[ci], s)
        o = _bdot(qe[ci], s) + _bdot(qkd[ci], vn)
        o_s[sl, :] = _fold4(o, CHUNK)
        s = s * jnp.exp(tot_r[ci]) + _bdot(kdt[ci], vn)
    s_scr[...] = s
    o = o_s[...]
    ms = _split_dot(o * o, ones_bd) * (1.0 / dk)
    y_ref[0] = (o * lax.rsqrt(ms + EPS) * og_ref[...] * _silu(z_ref[0])).astype(BF16)
    sout_ref[0] = s_scr[...]


def _gdn(qkv, z, small, hist, s0_bd, cw, alog, dtb, og, ones_bd, tril, masks, sel):
    bsz, l, w3 = qkv.shape
    wg = w3 // 3
    tl = min(256, l)
    kern = functools.partial(_gdn_kernel, tl=tl, wg=wg)
    bt = lambda b_, t_: (b_, t_, 0)
    b0 = lambda b_, t_: (b_, 0, 0)
    return pl.pallas_call(
        kern,
        out_shape=[jax.ShapeDtypeStruct((bsz, l, wg), BF16),
                   jax.ShapeDtypeStruct((bsz, wg, wg), F32)],
        grid=(bsz, l // tl),
        in_specs=[pl.BlockSpec((1, tl, w3), bt),
                  pl.BlockSpec((1, tl, wg), bt),
                  pl.BlockSpec((1, tl, LANES), bt),
                  pl.BlockSpec((1, SHORT_PAD, w3), b0),
                  pl.BlockSpec((1, wg, wg), b0),
                  _const_spec((SHORT_PAD, w3)),
                  _const_spec((1, LANES)), _const_spec((1, LANES)), _const_spec((1, wg)),
                  _const_spec((wg, wg)), _const_spec((CHUNK, CHUNK)),
                  _const_spec(masks.shape), _const_spec(sel.shape)],
        out_specs=[pl.BlockSpec((1, tl, wg), bt),
                   pl.BlockSpec((1, wg, wg), b0)],
        scratch_shapes=[pltpu.VMEM((SHORT_PAD + tl, w3), F32),
                        pltpu.VMEM((wg, wg), F32),
                        pltpu.VMEM((tl, wg), F32), pltpu.VMEM((tl, wg), F32), pltpu.VMEM((tl, wg), F32),
                        pltpu.VMEM((tl, LANES), F32), pltpu.VMEM((tl, LANES), F32),
                        pltpu.VMEM((tl, wg), F32)],
        compiler_params=_cparams(("parallel", "arbitrary")),
        name="gated_deltanet",
    )(qkv, z, small, hist, s0_bd, cw, alog, dtb, og, ones_bd, tril, masks, sel)


def _ssd_kernel(xbc_ref, z_ref, sm_ref, hist_ref, h0_ref, cw_ref, cb_ref, alog_ref, dtb_ref, dsk_ref, ng_ref,
                tril_ref, mask_ref, sel_ref, y_ref, hout_ref, xbuf, h_scr, x_s, b_s, c_s, dt_s, da_s, y_s, *, tl, wg):
    t = pl.program_id(1)
    n = wg

    @pl.when(t == 0)
    def _():
        h_scr[...] = h0_ref[0]

    c = _silu(_short_conv(xbc_ref, hist_ref, cw_ref, xbuf, t, tl) + cb_ref[...])
    xs = c[:, :wg]
    x_s[...] = xs
    b_s[...] = c[:, wg:wg + G_D * N_D]
    c_s[...] = c[:, wg + G_D * N_D:]
    dt = _softplus(sm_ref[0] + dtb_ref[...])
    dt_s[...] = dt
    da_s[...] = -jnp.exp(alog_ref[...]) * dt

    head, incl = mask_ref[0], mask_ref[1]
    sel_dt = sel_ref[2]
    tril = tril_ref[...]

    rep = H_D // G_D

    def per_head(a):
        return jnp.concatenate([a[:, N_D * (hh // rep):N_D * (hh // rep + 1)] for hh in range(H_D)], axis=0)

    sls = [slice(ci * CHUNK, (ci + 1) * CHUNK) for ci in range(tl // CHUNK)]
    acs = [_split_dot_rhs(tril, da_s[sl, :]) for sl in sls]
    dt_r = [_expand_col(dt_s[sl, :], sel_dt) for sl in sls]
    acs_r = [_expand_col(ac, sel_dt) for ac in acs]
    tot_r = [_expand_row(ac[CHUNK - 1:CHUNK, :], sel_dt) for ac in acs]
    xdt = [(_tile4(x_s[sl, :]) * head) * dr for sl, dr in zip(sls, dt_r)]
    cn = [per_head(c_s[sl, :]) for sl in sls]
    bn = [per_head(b_s[sl, :]) for sl in sls]
    scores = [_bdot_nt(cc, bc) * _decay_matrix(ar, incl, n) for cc, bc, ar in zip(cn, bn, acs_r)]
    y_diag = [_bdot(sc, xc) for sc, xc in zip(scores, xdt)]
    s_chunk = [_bdot(xc.T, bc * jnp.exp(tr - ar)) for xc, bc, tr, ar in zip(xdt, bn, tot_r, acs_r)]
    c_dec = [cc * jnp.exp(ar) for cc, ar in zip(cn, acs_r)]
    hs = h_scr[...]
    for ci, sl in enumerate(sls):
        y_bd = y_diag[ci] + _bdot_nt(c_dec[ci], hs) * head
        y_s[sl, :] = _fold4(y_bd, CHUNK)
        hs = hs * jnp.exp(tot_r[ci]) + s_chunk[ci]
    h_scr[...] = hs
    y = (y_s[...] + dsk_ref[...] * xs) * _silu(z_ref[0])
    gw = wg // G_D
    parts = []
    for g in range(G_D):
        yg = y[:, gw * g:gw * (g + 1)]
        ms = jnp.mean(yg * yg, axis=-1, keepdims=True)
        parts.append(yg * lax.rsqrt(ms + EPS))
    y_ref[0] = (jnp.concatenate(parts, axis=1) * ng_ref[...]).astype(BF16)
    hout_ref[0] = h_scr[...]


def _ssd(xbc, z, small, hist, h0, cw, cb, alog, dtb, dsk, ng, tril, masks, sel):
    bsz, l, wc = xbc.shape
    wg = z.shape[2]
    tl = min(256, l)
    kern = functools.partial(_ssd_kernel, tl=tl, wg=wg)
    bt = lambda b_, t_: (b_, t_, 0)
    b0 = lambda b_, t_: (b_, 0, 0)
    return pl.pallas_call(
        kern,
        out_shape=[jax.ShapeDtypeStruct((bsz, l, wg), BF16),
                   jax.ShapeDtypeStruct((bsz, wg, N_D), F32)],
        grid=(bsz, l // tl),
        in_specs=[pl.BlockSpec((1, tl, wc), bt),
                  pl.BlockSpec((1, tl, wg), bt),
                  pl.BlockSpec((1, tl, LANES), bt),
                  pl.BlockSpec((1, SHORT_PAD, wc), b0),
                  pl.BlockSpec((1, wg, N_D), b0),
                  _const_spec((SHORT_PAD, wc)), _const_spec((1, wc)),
                  _const_spec((1, LANES)), _const_spec((1, LANES)), _const_spec((1, wg)), _const_spec((1, wg)),
                  _const_spec((CHUNK, CHUNK)), _const_spec(masks.shape), _const_spec(sel.shape)],
        out_specs=[pl.BlockSpec((1, tl, wg), bt),
                   pl.BlockSpec((1, wg, N_D), b0)],
        scratch_shapes=[pltpu.VMEM((SHORT_PAD + tl, wc), F32),
                        pltpu.VMEM((wg, N_D), F32),
                        pltpu.VMEM((tl, wg), F32), pltpu.VMEM((tl, G_D * N_D), F32), pltpu.VMEM((tl, G_D * N_D), F32),
                        pltpu.VMEM((tl, LANES), F32), pltpu.VMEM((tl, LANES), F32),
                        pltpu.VMEM((tl, wg), F32)],
        compiler_params=_cparams(("parallel", "arbitrary")),
        name="ssd",
    )(xbc, z, small, hist, h0, cw, cb, alog, dtb, dsk, ng, tril, masks, sel)


def _route(logits):
    lane = lax.broadcasted_iota(jnp.int32, logits.shape, 1).astype(F32)
    big = float(LANES)
    lg = jnp.where(lane < E_GROUPS, logits, NEG_BIG)
    mg = jnp.max(lg, axis=-1, keepdims=True)
    sg = jnp.sum(jnp.exp(lg - mg), axis=-1, keepdims=True)
    gidx = jnp.min(jnp.where(lg == mg, lane, big), axis=-1, keepdims=True)
    w_grp = 1.0 / sg
    lo = E_GROUPS + E_PER_GROUP * gidx
    sel = (lane >= lo) & (lane < lo + E_PER_GROUP)
    le = jnp.where(sel, logits, NEG_BIG)
    me = jnp.max(le, axis=-1, keepdims=True)
    pe = jnp.where(sel, jnp.exp(le - me), 0.0)
    p_in = pe / jnp.sum(pe, axis=-1, keepdims=True)
    v1 = jnp.max(p_in, axis=-1, keepdims=True)
    i1 = jnp.min(jnp.where(sel & (p_in == v1), lane, big), axis=-1, keepdims=True)
    rest = sel & (lane != i1)
    p2 = jnp.where(rest, p_in, -1.0)
    v2 = jnp.max(p2, axis=-1, keepdims=True)
    i2 = jnp.min(jnp.where(rest & (p2 == v2), lane, big), axis=-1, keepdims=True)
    den = v1 + v2
    gate = jnp.where(lane == i1, v1 / den, 0.0) + jnp.where(lane == i2, v2 / den, 0.0)
    return gate * w_grp


def _moe_kernel(x_ref, ya_ref, yb_ref, yc_ref, yd_ref, wo_ref, g_ref, wr_ref, br_ref, w13_ref, w2_ref, o_ref, act_s,
                *, f, ne, wg):
    h = x_ref[...]
    for i, y_ref in enumerate((ya_ref, yb_ref, yc_ref, yd_ref)):
        h = h + jnp.dot(y_ref[...], wo_ref[wg * i:wg * (i + 1), :], preferred_element_type=F32)
    ms = jnp.mean(h * h, axis=-1, keepdims=True)
    xn = (h * lax.rsqrt(ms + EPS) * g_ref[...]).astype(BF16)
    gate = _route(jnp.dot(xn, wr_ref[...], preferred_element_type=F32) + br_ref[...])
    for e in range(ne):
        hid = jnp.dot(xn, w13_ref[e], preferred_element_type=F32)
        act = _silu(hid[:, :f]) * hid[:, f:]
        act_s[:, f * e:f * (e + 1)] = (act * gate[:, E_GROUPS + e:E_GROUPS + e + 1]).astype(BF16)
    o_ref[...] = h + jnp.dot(act_s[...], w2_ref[...], preferred_element_type=F32)


def _out_proj_moe(x2d, ys, w_out, g, wr, br, w13, w2s):
    t, d = x2d.shape
    wg = ys[0].shape[1]
    ne, _, f2 = w13.shape
    f = f2 // 2
    tm = min(512, t)
    row = lambda i: (i, 0)
    resident = dict(pipeline_mode=pl.Buffered(1))
    return pl.pallas_call(
        functools.partial(_moe_kernel, f=f, ne=ne, wg=wg),
        out_shape=jax.ShapeDtypeStruct((t, d), F32),
        grid=(t // tm,),
        in_specs=[pl.BlockSpec((tm, d), row)] + [pl.BlockSpec((tm, wg), row)] * 4 + [
                  pl.BlockSpec(w_out.shape, lambda i: (0, 0), **resident),
                  _const_spec((1, d)), _const_spec((d, LANES)), _const_spec((1, LANES)),
                  pl.BlockSpec((ne, d, f2), lambda i: (0, 0, 0), **resident),
                  pl.BlockSpec((ne * f, d), lambda i: (0, 0), **resident)],
        out_specs=pl.BlockSpec((tm, d), row),
        scratch_shapes=[pltpu.VMEM((tm, ne * f), BF16)],
        compiler_params=_cparams(("parallel",)),
        name="out_proj_moe",
    )(x2d, *ys, w_out, g, wr, br, w13, w2s)


def _block_diag_const(n, blk, val):
    r = np.arange(n)
    return jnp.asarray(np.where((r[:, None] // blk) == (r[None, :] // blk), val, 0.0), dtype=BF16)


def _rope_tables(pos, wg, dqk):
    rot = dqk // 4
    half = rot // 2
    inv = jnp.exp(jnp.arange(half, dtype=F32) * (-2.0 / rot) * math.log(ROPE_THETA))
    ang = pos.astype(F32)[:, None] * inv[None, :]
    cos, sin = jnp.cos(ang), jnp.sin(ang)
    l = pos.shape[0]
    ones = jnp.ones((l, dqk - rot), F32)
    zeros = jnp.zeros((l, dqk - rot), F32)
    cos_d = jnp.concatenate([cos, cos, ones], axis=1)
    sin_d = jnp.concatenate([-sin, sin, zeros], axis=1)
    reps = wg // dqk
    return jnp.tile(cos_d, (1, reps)), jnp.tile(sin_d, (1, reps))


def _pad_lanes(v, lane0):
    out = jnp.zeros((1, LANES), F32)
    return out.at[0, lane0:lane0 + v.shape[0]].set(v.astype(F32))


def _layer_params(l, p, wg, d):
    sizes = (2 * wg, 3 * wg, 3 * wg, wg, H_C, H_C, wg, wg + 2 * G_D * N_D, H_D)
    cuts = np.cumsum((0,) + sizes)
    w_in = p['w_in'][l]
    seg = lambda i: w_in[:, cuts[i]:cuts[i + 1]]
    n_small = H_C + H_C + H_D
    w_pad = jnp.concatenate([seg(0), seg(1), seg(2), seg(3), seg(6), seg(7), seg(4), seg(5), seg(8),
                             jnp.zeros((d, LANES - n_small), F32)], axis=1).astype(BF16)
    segs = (sizes[0], sizes[1], sizes[2], sizes[3], sizes[6], sizes[7], LANES)
    dqk = wg // (2 * H_B)
    row = lambda v: v.astype(F32).reshape(1, -1)
    wc = sizes[7]
    lp = dict(
        segs=segs, w_in=w_pad, norm_mix_g=row(p['norm_mix_g'][l]),
        w_out=p['w_out'][l].astype(BF16),
        conf_w=jnp.concatenate([p['conf_conv_w'][l], jnp.zeros((CONV_A_PAD - CONV_A, wg), F32)], axis=0),
        conf_b=row(p['conf_conv_b'][l]), conf_g=row(p['conf_ln_g'][l]), conf_beta=row(p['conf_ln_b'][l]),
        qg=row(jnp.tile(p['diff_qnorm_g'][l], wg // dqk)), kg=row(jnp.tile(p['diff_knorm_g'][l], wg // dqk)),
        lam_p=p['diff_lambda'][l].astype(F32), subln_g=row(p['diff_subln_g'][l]),
        gdn_w=jnp.concatenate([p['gdn_conv_w'][l], jnp.zeros((SHORT_PAD - CONV_SHORT, 3 * wg), F32)], axis=0),
        gdn_alog=_pad_lanes(p['gdn_A_log'][l], H_C), gdn_dtb=_pad_lanes(p['gdn_dt_bias'][l], H_C),
        gdn_og=row(jnp.tile(p['gdn_onorm_g'][l], H_C)),
        ssd_w=jnp.concatenate([p['ssd_conv_w'][l], jnp.zeros((SHORT_PAD - CONV_SHORT, wc), F32)], axis=0),
        ssd_b=row(p['ssd_conv_b'][l]),
        ssd_alog=_pad_lanes(p['ssd_A_log'][l], 2 * H_C), ssd_dtb=_pad_lanes(p['ssd_dt_bias'][l], 2 * H_C),
        ssd_dsk=row(jnp.repeat(p['ssd_D'][l], wg // H_D)), ssd_ng=row(p['ssd_norm_g'][l]),
        norm_ffn_g=row(p['norm_ffn_g'][l]),
        moe_wr=jnp.concatenate([p['moe_w_group'][l], p['moe_w_expert'][l],
                                jnp.zeros((d, LANES - E_GROUPS - N_EXPERTS), F32)], axis=1).astype(BF16),
        moe_br=jnp.concatenate([p['moe_b_group'][l], p['moe_b_expert'][l],
                                jnp.zeros((LANES - E_GROUPS - N_EXPERTS,), F32)]).reshape(1, LANES),
        moe_w13=jnp.concatenate([p['moe_w1'][l], p['moe_w3'][l]], axis=-1).astype(BF16),
        moe_w2=p['moe_w2'][l].astype(BF16).reshape(-1, d),
    )
    return lp


def _trunk_layer(x, lp, consts, pos, lam_init, buf_a, k_past, v_past, buf_c, s_c, buf_d, h_d):
    bsz, l, d = x.shape
    wg = d // N_MIXERS
    dqk = wg // (2 * H_B)
    dv = wg // H_B
    assert wg // H_C == CHUNK and wg // H_D == CHUNK and H_C == H_D == N_MIXERS
    assert l % CHUNK == 0 and l >= CONV_A_PAD
    x2d = x.reshape(bsz * l, d)
    u_a, u_b, u_c, u_cz, u_dz, u_d, u_s = _in_proj(x2d, lp['norm_mix_g'], lp['w_in'], lp['segs'])
    r3 = lambda a: a.reshape(bsz, l, a.shape[-1])
    u_a, u_b, u_c, u_cz, u_dz, u_d, u_s = map(r3, (u_a, u_b, u_c, u_cz, u_dz, u_d, u_s))

    cache_pad = jnp.concatenate([jnp.zeros((bsz, CONV_A_PAD - (CONV_A - 1), wg), F32), buf_a.astype(F32)], axis=1)
    y_a, tail_a = _conformer(u_a, cache_pad, lp['conf_w'], lp['conf_b'], lp['conf_g'], lp['conf_beta'],
                             consts['mean_a'])
    new_buf_a = tail_a[:, CONV_A_PAD - (CONV_A - 1):, :]

    cos_t, sin_t = _rope_tables(pos, wg, dqk)
    q_b, kt_b, v_b, k_new, v_new = _attn_prep(u_b, cos_t, sin_t, lp['qg'], lp['kg'], consts['mean_qk'],
                                              consts['eye'], dqk)
    if k_past is not None:
        p = k_past.shape[1]
        ktp = _transpose_keys(k_past.reshape(bsz, p, wg).astype(F32), consts['eye'])
        vp4 = v_past.reshape(bsz, p, H_B, dv).astype(BF16)
        one = jnp.zeros((bsz, p, H_B, dv), BF16).at[..., 0].set(1.0)
        vp = jnp.concatenate([vp4, one], axis=-1).reshape(bsz, p, 2 * wg)
    else:
        ktp, vp = None, None
    y_b = _flash(q_b, kt_b, v_b, ktp, vp, lp['lam_p'], lp['subln_g'], lam_init, dqk, dv)

    hist_c = jnp.concatenate([jnp.zeros((bsz, SHORT_PAD - (CONV_SHORT - 1), 3 * wg), F32), buf_c.astype(F32)], axis=1)
    s0_bd = jnp.tile(s_c.astype(F32).reshape(bsz, wg, wg // H_C), (1, 1, H_C)) * consts['head_mask']
    y_c, s_bd = _gdn(u_c, u_cz, u_s, hist_c, s0_bd, lp['gdn_w'], lp['gdn_alog'], lp['gdn_dtb'], lp['gdn_og'],
                     consts['ones_c'], consts['tril'], consts['bd_masks'], consts['sel'])
    dk = wg // H_C
    s_new = jnp.stack([s_bd[:, dk * h:dk * (h + 1), dk * h:dk * (h + 1)] for h in range(H_C)], axis=1)
    new_buf_c = u_c[:, l - (CONV_SHORT - 1):, :]

    wc = u_d.shape[-1]
    hist_d = jnp.concatenate([jnp.zeros((bsz, SHORT_PAD - (CONV_SHORT - 1), wc), F32), buf_d.astype(F32)], axis=1)
    h0 = h_d.astype(F32).reshape(bsz, wg, N_D)
    y_d, h_new = _ssd(u_d, u_dz, u_s, hist_d, h0, lp['ssd_w'], lp['ssd_b'], lp['ssd_alog'], lp['ssd_dtb'],
                      lp['ssd_dsk'], lp['ssd_ng'], consts['tril'], consts['bd_masks'], consts['sel'])
    h_new = h_new.reshape(bsz, H_D, wg // H_D, N_D)
    new_buf_d = u_d[:, l - (CONV_SHORT - 1):, :]

    r2 = lambda a: a.reshape(bsz * l, wg)
    y2d = _out_proj_moe(x2d, [r2(y_a), r2(y_b), r2(y_c), r2(y_d)], lp['w_out'], lp['norm_ffn_g'], lp['moe_wr'],
                        lp['moe_br'], lp['moe_w13'], lp['moe_w2'])
    y = y2d.reshape(bsz, l, d)
    k_new = k_new.reshape(bsz, l, H_B, 2, dqk)
    v_new = v_new.reshape(bsz, l, H_B, dv)
    return y, new_buf_a, k_new, v_new, new_buf_c, s_new, new_buf_d, h_new


@jax.jit
def _forward(x_prompt, x_sample, cache_conv_conformer, cache_k_diff, cache_v_diff, cache_conv_delta,
             state_delta, cache_conv_ssd, state_ssd, params):
    bp, lp_, d = x_prompt.shape
    ls = x_sample.shape[1]
    depth = params['w_in'].shape[0]
    past_len = cache_k_diff.shape[2]
    wg = d // N_MIXERS
    dqk = wg // (2 * H_B)
    consts = dict(
        mean_a=_block_diag_const(wg, wg // A_GROUPS, 1.0 / (wg // A_GROUPS)),
        mean_qk=_block_diag_const(wg, dqk, 1.0 / dqk),
        ones_c=_block_diag_const(wg, wg // H_C, 1.0),
        eye=jnp.eye(wg, dtype=BF16),
        tril=jnp.asarray(np.tril(np.ones((CHUNK, CHUNK))), dtype=BF16),
        head_mask=jnp.asarray(np.kron(np.eye(H_C), np.ones((wg // H_C, wg // H_C))), dtype=F32),
        bd_masks=_bd_mask_consts(wg),
        sel=_sel_consts(wg, (0, H_C, 2 * H_C)),
    )
    pos_p = jnp.arange(lp_, dtype=jnp.int32)
    pos_s = past_len + jnp.arange(ls, dtype=jnp.int32)
    y_p, y_s = x_prompt, x_sample
    new_p = [[] for _ in range(7)]
    new_s = [[] for _ in range(7)]
    wc = wg + 2 * G_D * N_D
    for l in range(depth):
        lam_init = 0.8 - 0.6 * math.exp(-0.3 * l)
        lp = _layer_params(l, params, wg, d)
        y_p, *st_p = _trunk_layer(
            y_p, lp, consts, pos_p, lam_init,
            jnp.zeros((bp, CONV_A - 1, wg), F32), None, None,
            jnp.zeros((bp, CONV_SHORT - 1, 3 * wg), F32), jnp.zeros((bp, H_C, wg // H_C, wg // H_C), F32),
            jnp.zeros((bp, CONV_SHORT - 1, wc), F32), jnp.zeros((bp, H_D, wg // H_D, N_D), F32))
        y_s, *st_s = _trunk_layer(
            y_s, lp, consts, pos_s, lam_init,
            cache_conv_conformer[l], cache_k_diff[l], cache_v_diff[l],
            cache_conv_delta[l], state_delta[l], cache_conv_ssd[l], state_ssd[l])
        for i in range(7):
            new_p[i].append(st_p[i])
            new_s[i].append(st_s[i])
    outs_p = [jnp.stack(s, axis=0) for s in new_p]
    outs_s = [jnp.stack(s, axis=0) for s in new_s]
    return (y_p, y_s, *outs_p, *outs_s)


def kernel(x_prompt, x_sample, cache_conv_conformer, cache_k_diff, cache_v_diff, cache_conv_delta, state_delta, cache_conv_ssd, state_ssd, norm_mix_g, w_in, w_out, conf_conv_w, conf_conv_b, conf_ln_g, conf_ln_b, diff_qnorm_g, diff_knorm_g, diff_lambda, diff_subln_g, gdn_conv_w, gdn_A_log, gdn_dt_bias, gdn_onorm_g, ssd_conv_w, ssd_conv_b, ssd_A_log, ssd_dt_bias, ssd_D, ssd_norm_g, norm_ffn_g, moe_w_group, moe_b_group, moe_w_expert, moe_b_expert, moe_w1, moe_w3, moe_w2):
    params = dict(
        norm_mix_g=norm_mix_g, w_in=w_in, w_out=w_out, conf_conv_w=conf_conv_w, conf_conv_b=conf_conv_b,
        conf_ln_g=conf_ln_g, conf_ln_b=conf_ln_b, diff_qnorm_g=diff_qnorm_g, diff_knorm_g=diff_knorm_g,
        diff_lambda=diff_lambda, diff_subln_g=diff_subln_g, gdn_conv_w=gdn_conv_w, gdn_A_log=gdn_A_log,
        gdn_dt_bias=gdn_dt_bias, gdn_onorm_g=gdn_onorm_g, ssd_conv_w=ssd_conv_w, ssd_conv_b=ssd_conv_b,
        ssd_A_log=ssd_A_log, ssd_dt_bias=ssd_dt_bias, ssd_D=ssd_D, ssd_norm_g=ssd_norm_g,
        norm_ffn_g=norm_ffn_g, moe_w_group=moe_w_group, moe_b_group=moe_b_group, moe_w_expert=moe_w_expert,
        moe_b_expert=moe_b_expert, moe_w1=moe_w1, moe_w3=moe_w3, moe_w2=moe_w2)
    return _forward(x_prompt, x_sample, cache_conv_conformer, cache_k_diff, cache_v_diff, cache_conv_delta,
                    state_delta, cache_conv_ssd, state_ssd, params)
```

```python
import functools
import math

import numpy as np
import jax
import jax.numpy as jnp
from jax import lax
from jax.experimental import pallas as pl
from jax.experimental.pallas import tpu as pltpu

F32 = jnp.float32
BF16 = jnp.bfloat16

CHUNK = 64
N_MIXERS = 4
CONV_A = 31
A_GROUPS = 4
H_B = 4
ROPE_THETA = 500000.0
H_C = 4
CONV_SHORT = 4
H_D = 4
G_D = 2
N_D = 128
E_GROUPS = 4
E_PER_GROUP = 4
N_EXPERTS = E_GROUPS * E_PER_GROUP
EPS = 1e-6
NEG_BIG = -1e30

LANES = 128
SUBLANES = 8
CONV_A_PAD = 32
SHORT_PAD = 8
VMEM_LIMIT = 56 * 1024 * 1024
HEADS_PER_GROUP = 2


def _bdot(a, b):
    return jnp.dot(a.astype(BF16), b.astype(BF16), preferred_element_type=F32)


def _bdot_nt(a, b):
    return lax.dot_general(a.astype(BF16), b.astype(BF16), (((1,), (1,)), ((), ())),
                           preferred_element_type=F32)


def _dot3(a, b):
    a_hi = a.astype(BF16)
    a_lo = (a - a_hi.astype(F32)).astype(BF16)
    b_hi = b.astype(BF16)
    b_lo = (b - b_hi.astype(F32)).astype(BF16)
    return (jnp.dot(a_hi, b_hi, preferred_element_type=F32) + jnp.dot(a_hi, b_lo, preferred_element_type=F32)
            + jnp.dot(a_lo, b_hi, preferred_element_type=F32))


def _split_dot(x, m, n_split=3):
    acc = None
    r = x
    for i in range(n_split):
        p = r.astype(BF16)
        d = jnp.dot(p, m, preferred_element_type=F32)
        acc = d if acc is None else acc + d
        if i + 1 < n_split:
            r = r - p.astype(F32)
    return acc


def _split_dot_rhs(m, x, n_split=3):
    acc = None
    r = x
    for i in range(n_split):
        p = r.astype(BF16)
        d = jnp.dot(m, p, preferred_element_type=F32)
        acc = d if acc is None else acc + d
        if i + 1 < n_split:
            r = r - p.astype(F32)
    return acc


def _sigmoid(x):
    return 1.0 / (1.0 + jnp.exp(-x))


def _silu(x):
    return x * _sigmoid(x)


def _softplus(x):
    return jnp.maximum(x, 0.0) + jnp.log1p(jnp.exp(-jnp.abs(x)))


def _tile4(a):
    return jnp.concatenate([a, a, a, a], axis=0)


def _fold4(a, n):
    return a[0:n] + a[n:2 * n] + a[2 * n:3 * n] + a[3 * n:4 * n]


def _cparams(sem):
    return pltpu.CompilerParams(dimension_semantics=sem, vmem_limit_bytes=VMEM_LIMIT)


def _const_spec(shape):
    nd = len(shape)
    return pl.BlockSpec(shape, lambda *_: (0,) * nd)


def _in_proj_kernel(x_ref, g_ref, w_ref, *out_refs, segs):
    x = x_ref[...]
    ms = jnp.mean(x * x, axis=-1, keepdims=True)
    xn = (x * lax.rsqrt(ms + EPS) * g_ref[...]).astype(BF16)
    off = 0
    for o_ref, n in zip(out_refs, segs):
        o_ref[...] = jnp.dot(xn, w_ref[:, off:off + n], preferred_element_type=F32)
        off += n


def _in_proj(x2d, g, w_pad, segs):
    t, d = x2d.shape
    tm = min(512, t)
    n_all = sum(segs)
    return pl.pallas_call(
        functools.partial(_in_proj_kernel, segs=segs),
        out_shape=[jax.ShapeDtypeStruct((t, n), F32) for n in segs],
        grid=(t // tm,),
        in_specs=[pl.BlockSpec((tm, d), lambda i: (i, 0)),
                  _const_spec((1, d)),
                  _const_spec((d, n_all))],
        out_specs=[pl.BlockSpec((tm, n), lambda i: (i, 0)) for n in segs],
        compiler_params=_cparams(("parallel",)),
        name="in_proj",
    )(x2d, g, w_pad)


def _conf_kernel(u_ref, cache_ref, w_ref, b_ref, g_ref, beta_ref, pm_ref, y_ref, tail_ref, abuf, shifted,
                 *, tl, wg):
    t = pl.program_id(1)

    @pl.when(t == 0)
    def _():
        abuf[0:CONV_A_PAD, :] = cache_ref[0]

    u = u_ref[0]
    a = u[:, :wg] * _sigmoid(u[:, wg:])
    abuf[CONV_A_PAD:CONV_A_PAD + tl, :] = a
    span = tl + CONV_A_PAD - SUBLANES
    for r in range(1, SUBLANES):
        shifted[r - 1] = abuf[r:r + span, :]
    acc = jnp.zeros((tl, wg), F32) + b_ref[...]
    base = CONV_A_PAD - (CONV_A - 1)
    for j in range(CONV_A):
        q, r = divmod(base + j, SUBLANES)
        win = abuf[SUBLANES * q:SUBLANES * q + tl, :] if r == 0 else shifted[r - 1, SUBLANES * q:SUBLANES * q + tl, :]
        acc = acc + w_ref[j:j + 1, :] * win
    pm = pm_ref[...]
    mu = _split_dot(acc, pm)
    dlt = acc - mu
    var = _split_dot(dlt * dlt, pm)
    cn = dlt * lax.rsqrt(var + EPS) * g_ref[...] + beta_ref[...]
    y_ref[0] = _silu(cn).astype(BF16)
    tail = abuf[tl:tl + CONV_A_PAD, :]
    tail_ref[0] = tail
    abuf[0:CONV_A_PAD, :] = tail


def _conformer(u_a, cache_pad, w_pad, b, g, beta, pm):
    bsz, l, two_wg = u_a.shape
    wg = two_wg // 2
    tl = min(512, l)
    return pl.pallas_call(
        functools.partial(_conf_kernel, tl=tl, wg=wg),
        out_shape=[jax.ShapeDtypeStruct((bsz, l, wg), BF16),
                   jax.ShapeDtypeStruct((bsz, CONV_A_PAD, wg), F32)],
        grid=(bsz, l // tl),
        in_specs=[pl.BlockSpec((1, tl, two_wg), lambda b_, t_: (b_, t_, 0)),
                  pl.BlockSpec((1, CONV_A_PAD, wg), lambda b_, t_: (b_, 0, 0)),
                  _const_spec((CONV_A_PAD, wg)),
                  _const_spec((1, wg)), _const_spec((1, wg)), _const_spec((1, wg)),
                  _const_spec((wg, wg))],
        out_specs=[pl.BlockSpec((1, tl, wg), lambda b_, t_: (b_, t_, 0)),
                   pl.BlockSpec((1, CONV_A_PAD, wg), lambda b_, t_: (b_, 0, 0))],
        scratch_shapes=[pltpu.VMEM((CONV_A_PAD + tl, wg), F32),
                        pltpu.VMEM((SUBLANES - 1, tl + CONV_A_PAD - SUBLANES, wg), F32)],
        compiler_params=_cparams(("parallel", "arbitrary")),
        name="conformer",
    )(u_a, cache_pad, w_pad, b, g, beta, pm)


def _attn_prep_kernel(qkv_ref, cos_ref, sin_ref, qg_ref, kg_ref, pm_ref, eye_ref,
                      q_ref, kt_ref, vb_ref, kn_ref, vn_ref, *, wg, dqk):
    qkv = qkv_ref[0]
    cos = cos_ref[...]
    sin = sin_ref[...]
    pm = pm_ref[...]
    lane = lax.broadcasted_iota(jnp.int32, (1, wg), 1)
    low = (lane % dqk) < (dqk // 8)

    def norm_rope(x, g):
        ms = _split_dot(x * x, pm)
        xn = x * lax.rsqrt(ms + EPS) * g
        rot = jnp.where(low, pltpu.roll(xn, wg - dqk // 8, 1), pltpu.roll(xn, dqk // 8, 1))
        return xn * cos + rot * sin

    q = norm_rope(qkv[:, :wg], qg_ref[...])
    k = norm_rope(qkv[:, wg:2 * wg], kg_ref[...])
    v = qkv[:, 2 * wg:]
    q_ref[0] = q.astype(BF16)
    kn_ref[0] = k
    vn_ref[0] = v
    dv = 2 * dqk
    pad = jnp.where(lax.broadcasted_iota(jnp.int32, (v.shape[0], dv), 1) == 0, 1.0, 0.0)
    vb_ref[0] = jnp.concatenate(
        [blk for h in range(H_B) for blk in (v[:, dv * h:dv * (h + 1)], pad)], axis=1).astype(BF16)
    kt_ref[0] = _bdot_nt(eye_ref[...], k).astype(BF16)


def _attn_prep(qkv, cos_t, sin_t, qg, kg, pm, eye, dqk):
    bsz, l, w3 = qkv.shape
    wg = w3 // 3
    tl = min(512, l)
    kern = functools.partial(_attn_prep_kernel, wg=wg, dqk=dqk)
    return pl.pallas_call(
        kern,
        out_shape=[jax.ShapeDtypeStruct((bsz, l, wg), BF16),
                   jax.ShapeDtypeStruct((bsz, wg, l), BF16),
                   jax.ShapeDtypeStruct((bsz, l, 2 * wg), BF16),
                   jax.ShapeDtypeStruct((bsz, l, wg), F32),
                   jax.ShapeDtypeStruct((bsz, l, wg), F32)],
        grid=(l // tl, bsz),
        in_specs=[pl.BlockSpec((1, tl, w3), lambda t_, b_: (b_, t_, 0)),
                  pl.BlockSpec((tl, wg), lambda t_, b_: (t_, 0)),
                  pl.BlockSpec((tl, wg), lambda t_, b_: (t_, 0)),
                  _const_spec((1, wg)), _const_spec((1, wg)),
                  _const_spec((wg, wg)), _const_spec((wg, wg))],
        out_specs=[pl.BlockSpec((1, tl, wg), lambda t_, b_: (b_, t_, 0)),
                   pl.BlockSpec((1, wg, tl), lambda t_, b_: (b_, 0, t_)),
                   pl.BlockSpec((1, tl, 2 * wg), lambda t_, b_: (b_, t_, 0)),
                   pl.BlockSpec((1, tl, wg), lambda t_, b_: (b_, t_, 0)),
                   pl.BlockSpec((1, tl, wg), lambda t_, b_: (b_, t_, 0))],
        compiler_params=_cparams(("parallel", "parallel")),
        name="attn_prep",
    )(qkv, cos_t, sin_t, qg, kg, pm, eye)


def _kt_kernel(k_ref, eye_ref, kt_ref):
    kt_ref[0] = _bdot_nt(eye_ref[...], k_ref[0]).astype(BF16)


def _transpose_keys(k, eye):
    bsz, p, w = k.shape
    tp = min(512, p)
    return pl.pallas_call(
        _kt_kernel,
        out_shape=jax.ShapeDtypeStruct((bsz, w, p), BF16),
        grid=(bsz, p // tp),
        in_specs=[pl.BlockSpec((1, tp, w), lambda b_, t_: (b_, t_, 0)), _const_spec((w, w))],
        out_specs=pl.BlockSpec((1, w, tp), lambda b_, t_: (b_, 0, t_)),
        compiler_params=_cparams(("parallel", "parallel")),
        name="transpose_keys",
    )(k, eye)


def _flash_kernel(*refs, tq, tkb, tkp, n_past, nq, dqk, dv, lam_init):
    if n_past:
        q_ref, kt_ref, v_ref, ktp_ref, vp_ref, lam_ref, sg_ref, o_ref = refs
    else:
        q_ref, kt_ref, v_ref, lam_ref, sg_ref, o_ref = refs
    i = pl.program_id(1)
    lp = lam_ref[...]
    lam = (jnp.exp(jnp.sum(lp[0:1] * lp[1:2], axis=-1, keepdims=True))
           - jnp.exp(jnp.sum(lp[2:3] * lp[3:4], axis=-1, keepdims=True)) + lam_init)
    sg = sg_ref[...]
    row_c = lax.broadcasted_iota(jnp.int32, (tq, tq), 0) // CHUNK
    col_c = lax.broadcasted_iota(jnp.int32, (tq, tq), 1) // CHUNK
    diag_mask = col_c <= row_c
    hw = 2 * dqk
    vw = 2 * dv
    c2 = (dqk ** -0.5) * math.log2(math.e)

    def update(qm, ktm, vt, carry, mask):
        m, acc = carry
        s = jnp.dot(qm, ktm, preferred_element_type=F32)
        if mask is not None:
            s = jnp.where(mask, s, NEG_BIG)
        m_new = jnp.maximum(m, jnp.max(s, axis=-1, keepdims=True))
        alpha = jnp.exp2((m - m_new) * c2)
        p = jnp.exp2((s - m_new) * c2)
        acc = alpha * acc + jnp.dot(p.astype(BF16), vt, preferred_element_type=F32)
        return m_new, acc

    heads = range(H_B)
    rows = [slice(hw * h, hw * (h + 1)) for h in heads]
    cols = [slice(vw * h, vw * (h + 1)) for h in heads]
    qs = []
    for h in heads:
        qh = q_ref[0, :, rows[h]]
        qs.append((qh[:, :dqk], qh[:, dqk:]))

    def finish(o):
        ms = jnp.mean(o * o, axis=-1, keepdims=True)
        return o * lax.rsqrt(ms + EPS) * sg * (1.0 - lam_init)

    if nq == 1:
        outs = []
        for h in heads:
            kts = [ktp_ref[0, rows[h], :]] if n_past else []
            vts = [vp_ref[0, :, cols[h]]] if n_past else []
            kt = jnp.concatenate(kts + [kt_ref[0, rows[h], :]], axis=1)
            vt = jnp.concatenate(vts + [v_ref[0, :, cols[h]]], axis=0)
            n_old = kt.shape[1] - tq
            vis = jnp.concatenate([jnp.full((tq, n_old), True), diag_mask], axis=1) if n_old else diag_mask

            def probs(qm, ktm):
                s = jnp.dot(qm, ktm, preferred_element_type=F32) * (dqk ** -0.5)
                s = jnp.where(vis, s, NEG_BIG)
                e = jnp.exp(s - jnp.max(s, axis=-1, keepdims=True))
                return e / jnp.sum(e, axis=-1, keepdims=True)

            pd = probs(qs[h][0], kt[:dqk]) - lam * probs(qs[h][1], kt[dqk:])
            outs.append(finish(jnp.dot(pd.astype(BF16), vt, preferred_element_type=F32)[:, :dv]))
        o_ref[0] = jnp.concatenate(outs, axis=1).astype(BF16)
        return

    def tile_group(carries, group, kt_of, vt_of, mask):
        chains = [(h, c) for h in group for c in range(2)]
        kts = {h: kt_of(rows[h]) for h in group}
        vts = {h: vt_of(cols[h]) for h in group}
        s = [jnp.dot(qs[h][c], kts[h][dqk * c:dqk * (c + 1)], preferred_element_type=F32) for h, c in chains]
        if mask is not None:
            s = [jnp.where(mask, x, NEG_BIG) for x in s]
        m_old = [carries[h][c][0] for h, c in chains]
        m_new = [jnp.maximum(m, jnp.max(x, axis=-1, keepdims=True)) for m, x in zip(m_old, s)]
        p = [jnp.exp2((x - m) * c2).astype(BF16) for x, m in zip(s, m_new)]
        pv = [jnp.dot(x, vts[h], preferred_element_type=F32) for x, (h, c) in zip(p, chains)]
        acc = [jnp.exp2((mo - mn) * c2) * carries[h][c][1] + y
               for mo, mn, y, (h, c) in zip(m_old, m_new, pv, chains)]
        return [((m_new[2 * g], acc[2 * g]), (m_new[2 * g + 1], acc[2 * g + 1])) for g in range(len(group))]

    def tile(carries, kt_of, vt_of, mask):
        out = []
        for g0 in range(0, H_B, HEADS_PER_GROUP):
            out += tile_group(carries, list(range(g0, g0 + HEADS_PER_GROUP)), kt_of, vt_of, mask)
        return tuple(out)

    init = (jnp.full((tq, 1), NEG_BIG, F32), jnp.zeros((tq, vw), F32))
    carries = tuple((init, init) for _ in heads)
    for j in range(n_past):
        carries = tile(carries, lambda r: ktp_ref[0, r, j * tkp:(j + 1) * tkp],
                       lambda c: vp_ref[0, j * tkp:(j + 1) * tkp, c], None)

    start = i * tq
    n_big = start // tkb if tkb > tq else 0

    def keys(off, width):
        return (lambda r: kt_ref[0, r, pl.ds(off, width)]), (lambda c: v_ref[0, pl.ds(off, width), c])

    def body_big(j, carries):
        return tile(carries, *keys(pl.multiple_of(j * tkb, tkb), tkb), None)

    n_small = (start - n_big * tkb) // tq

    def body_small(j, carries):
        bump = jnp.where(j == n_small, 0, tq // CHUNK)
        return tile(carries, *keys(pl.multiple_of(n_big * tkb + j * tq, tq), tq), col_c <= row_c + bump)

    if tkb > tq:
        carries = lax.fori_loop(0, n_big, body_big, carries)
    carries = lax.fori_loop(0, n_small + 1, body_small, carries)

    outs = []
    for h in heads:
        (_, a0), (_, a1) = carries[h]
        outs.append(finish(a0[:, :dv] * (1.0 / a0[:, dv:dv + 1]) - lam * (a1[:, :dv] * (1.0 / a1[:, dv:dv + 1]))))
    o_ref[0] = jnp.concatenate(outs, axis=1).astype(BF16)


def _flash(q, kt, v1, ktp, vp1, lam_p, sg, lam_init, dqk, dv):
    bsz, l, wg = q.shape
    vw_all = v1.shape[2]
    tq = min(512, l)
    tkb = min(1024, l)
    nq = l // tq
    n_past = 0
    tkp = 0
    args = [q, kt, v1]
    in_specs = [pl.BlockSpec((1, tq, wg), lambda b_, i_: (b_, i_, 0)),
                pl.BlockSpec((1, wg, l), lambda b_, i_: (b_, 0, 0)),
                pl.BlockSpec((1, l, vw_all), lambda b_, i_: (b_, 0, 0))]
    if ktp is not None:
        p = ktp.shape[2]
        tkp = min(2048, p)
        n_past = p // tkp
        args += [ktp, vp1]
        in_specs += [pl.BlockSpec((1, wg, p), lambda b_, i_: (b_, 0, 0)),
                     pl.BlockSpec((1, p, vw_all), lambda b_, i_: (b_, 0, 0))]
    args += [lam_p, sg]
    in_specs += [_const_spec(lam_p.shape), _const_spec(sg.shape)]
    kern = functools.partial(_flash_kernel, tq=tq, tkb=tkb, tkp=tkp, n_past=n_past, nq=nq, dqk=dqk, dv=dv,
                             lam_init=lam_init)
    return pl.pallas_call(
        kern,
        out_shape=jax.ShapeDtypeStruct((bsz, l, wg), BF16),
        grid=(bsz, nq),
        in_specs=in_specs,
        out_specs=pl.BlockSpec((1, tq, wg), lambda b_, i_: (b_, i_, 0)),
        compiler_params=_cparams(("parallel", "parallel")),
        name="flash_diff_attn",
    )(*args)


def _short_conv(x_ref, hist_ref, w_ref, xbuf, t, tl):
    @pl.when(t == 0)
    def _():
        xbuf[0:SHORT_PAD, :] = hist_ref[0]

    xbuf[SHORT_PAD:SHORT_PAD + tl, :] = x_ref[0]
    base = SHORT_PAD - (CONV_SHORT - 1)
    acc = w_ref[0:1, :] * xbuf[base:base + tl, :]
    for j in range(1, CONV_SHORT):
        acc = acc + w_ref[j:j + 1, :] * xbuf[base + j:base + j + tl, :]
    xbuf[0:SHORT_PAD, :] = xbuf[tl:tl + SHORT_PAD, :]
    return acc


def _bd_mask_consts(n):
    r = np.arange(n)
    head = (r[:, None] // CHUNK) == (r[None, :] // CHUNK)
    incl = head & (r[None, :] <= r[:, None])
    strict = head & (r[None, :] < r[:, None])
    return jnp.asarray(np.stack([head, incl, strict, np.eye(n, dtype=bool)]), dtype=F32)


def _sel_consts(n, lane0s):
    r = np.arange(n)[:, None] // CHUNK
    ln = np.arange(LANES)[None, :]
    return jnp.asarray(np.stack([ln == r + l0 for l0 in lane0s]), dtype=F32)


def _expand_col(x4, sel):
    return jnp.sum(_tile4(x4) * sel, axis=-1, keepdims=True)


def _expand_row(x1, sel):
    return jnp.sum(jnp.broadcast_to(x1, sel.shape) * sel, axis=-1, keepdims=True)


def _decay_matrix(cum_r, incl, n):
    cm = jnp.broadcast_to(cum_r, (n, n))
    return jnp.exp(jnp.minimum(cm - cm.T, 0.0)) * incl


def _gdn_kernel(qkv_ref, z_ref, sm_ref, hist_ref, s0_ref, cw_ref, alog_ref, dtb_ref, og_ref, ones_ref, tril_ref,
                mask_ref, sel_ref, y_ref, sout_ref, xbuf, s_scr, q_s, k_s, v_s, b_s, g_s, o_s, *, tl, wg):
    t = pl.program_id(1)
    n = wg

    @pl.when(t == 0)
    def _():
        s_scr[...] = s0_ref[0]

    c = _silu(_short_conv(qkv_ref, hist_ref, cw_ref, xbuf, t, tl))
    ones_bd = ones_ref[...]
    q = c[:, :wg]
    k = c[:, wg:2 * wg]
    dk = wg // H_C
    q_s[...] = q * lax.rsqrt(_split_dot(q * q, ones_bd) + EPS) * (dk ** -0.5)
    k_s[...] = k * lax.rsqrt(_split_dot(k * k, ones_bd) + EPS)
    v_s[...] = c[:, 2 * wg:]
    sm = sm_ref[0]
    b_s[...] = _sigmoid(sm)
    g_s[...] = -jnp.exp(alog_ref[...]) * _softplus(sm + dtb_ref[...])

    head, incl, strict, eye_f = mask_ref[0], mask_ref[1], mask_ref[2], mask_ref[3]
    sel_b, sel_g = sel_ref[0], sel_ref[1]
    tril = tril_ref[...]

    sls = [slice(ci * CHUNK, (ci + 1) * CHUNK) for ci in range(tl // CHUNK)]
    cum = [_split_dot_rhs(tril, g_s[sl, :]) for sl in sls]
    beta_r = [_expand_col(b_s[sl, :], sel_b) for sl in sls]
    cum_r = [_expand_col(cm, sel_g) for cm in cum]
    tot_r = [_expand_row(cm[CHUNK - 1:CHUNK, :], sel_g) for cm in cum]
    kx = [_tile4(k_s[sl, :]) * head for sl in sls]
    qx = [_tile4(q_s[sl, :]) * head for sl in sls]
    vx = [_tile4(v_s[sl, :]) * head for sl in sls]
    dm = [_decay_matrix(cr, incl, n) for cr in cum_r]
    a = [(br * _bdot_nt(kc, kc) * dc) * strict for br, kc, dc in zip(beta_r, kx, dm)]
    x = [eye_f - ac for ac in a]
    p = a
    for _ in range(int(math.log2(CHUNK)) - 2):
        p = [_bdot(pc, pc) for pc in p]
        x = [xc + _bdot(xc, pc) for xc, pc in zip(x, p)]
    x = [xc + _bdot(xc, eye_f - xc - _dot3(ac, xc)) for xc, ac in zip(x, a)]
    ecum = [jnp.exp(cr) for cr in cum_r]
    u = [_dot3(xc, br * vc) for xc, br, vc in zip(x, beta_r, vx)]
    w = [_dot3(xc, (br * ec) * kc) for xc, br, ec, kc in zip(x, beta_r, ecum, kx)]
    qe = [qc * ec for qc, ec in zip(qx, ecum)]
    qkd = [_bdot_nt(qc, kc) * dc for qc, kc, dc in zip(qx, kx, dm)]
    kdt = [(kc * jnp.exp(tr - cr)).T for kc, tr, cr in zip(kx, tot_r, cum_r)]
    s = s_scr[...]
    for ci, sl in enumerate(sls):
        vn = u[ci] - _bdot(w[ci], s)
        o = _bdot(qe[ci], s) + _bdot(qkd[ci], vn)
        o_s[sl, :] = _fold4(o, CHUNK)
        s = s * jnp.exp(tot_r[ci]) + _bdot(kdt[ci], vn)
    s_scr[...] = s
    o = o_s[...]
    ms = _split_dot(o * o, ones_bd) * (1.0 / dk)
    y_ref[0] = (o * lax.rsqrt(ms + EPS) * og_ref[...] * _silu(z_ref[0])).astype(BF16)
    sout_ref[0] = s_scr[...]


def _gdn(qkv, z, small, hist, s0_bd, cw, alog, dtb, og, ones_bd, tril, masks, sel):
    bsz, l, w3 = qkv.shape
    wg = w3 // 3
    tl = min(512, l)
    kern = functools.partial(_gdn_kernel, tl=tl, wg=wg)
    bt = lambda b_, t_: (b_, t_, 0)
    b0 = lambda b_, t_: (b_, 0, 0)
    return pl.pallas_call(
        kern,
        out_shape=[jax.ShapeDtypeStruct((bsz, l, wg), BF16),
                   jax.ShapeDtypeStruct((bsz, wg, wg), F32)],
        grid=(bsz, l // tl),
        in_specs=[pl.BlockSpec((1, tl, w3), bt),
                  pl.BlockSpec((1, tl, wg), bt),
                  pl.BlockSpec((1, tl, LANES), bt),
                  pl.BlockSpec((1, SHORT_PAD, w3), b0),
                  pl.BlockSpec((1, wg, wg), b0),
                  _const_spec((SHORT_PAD, w3)),
                  _const_spec((1, LANES)), _const_spec((1, LANES)), _const_spec((1, wg)),
                  _const_spec((wg, wg)), _const_spec((CHUNK, CHUNK)),
                  _const_spec(masks.shape), _const_spec(sel.shape)],
        out_specs=[pl.BlockSpec((1, tl, wg), bt),
                   pl.BlockSpec((1, wg, wg), b0)],
        scratch_shapes=[pltpu.VMEM((SHORT_PAD + tl, w3), F32),
                        pltpu.VMEM((wg, wg), F32),
                        pltpu.VMEM((tl, wg), F32), pltpu.VMEM((tl, wg), F32), pltpu.VMEM((tl, wg), F32),
                        pltpu.VMEM((tl, LANES), F32), pltpu.VMEM((tl, LANES), F32),
                        pltpu.VMEM((tl, wg), F32)],
        compiler_params=_cparams(("parallel", "arbitrary")),
        name="gated_deltanet",
    )(qkv, z, small, hist, s0_bd, cw, alog, dtb, og, ones_bd, tril, masks, sel)


def _ssd_kernel(xbc_ref, z_ref, sm_ref, hist_ref, h0_ref, cw_ref, cb_ref, alog_ref, dtb_ref, dsk_ref, ng_ref,
                tril_ref, mask_ref, sel_ref, y_ref, hout_ref, xbuf, h_scr, x_s, b_s, c_s, dt_s, da_s, y_s, *, tl, wg):
    t = pl.program_id(1)
    n = wg

    @pl.when(t == 0)
    def _():
        h_scr[...] = h0_ref[0]

    c = _silu(_short_conv(xbc_ref, hist_ref, cw_ref, xbuf, t, tl) + cb_ref[...])
    xs = c[:, :wg]
    x_s[...] = xs
    b_s[...] = c[:, wg:wg + G_D * N_D]
    c_s[...] = c[:, wg + G_D * N_D:]
    dt = _softplus(sm_ref[0] + dtb_ref[...])
    dt_s[...] = dt
    da_s[...] = -jnp.exp(alog_ref[...]) * dt

    head, incl = mask_ref[0], mask_ref[1]
    sel_dt = sel_ref[2]
    tril = tril_ref[...]

    rep = H_D // G_D

    def per_head(a):
        return jnp.concatenate([a[:, N_D * (hh // rep):N_D * (hh // rep + 1)] for hh in range(H_D)], axis=0)

    sls = [slice(ci * CHUNK, (ci + 1) * CHUNK) for ci in range(tl // CHUNK)]
    acs = [_split_dot_rhs(tril, da_s[sl, :]) for sl in sls]
    dt_r = [_expand_col(dt_s[sl, :], sel_dt) for sl in sls]
    acs_r = [_expand_col(ac, sel_dt) for ac in acs]
    tot_r = [_expand_row(ac[CHUNK - 1:CHUNK, :], sel_dt) for ac in acs]
    xdt = [(_tile4(x_s[sl, :]) * head) * dr for sl, dr in zip(sls, dt_r)]
    cn = [per_head(c_s[sl, :]) for sl in sls]
    bn = [per_head(b_s[sl, :]) for sl in sls]
    scores = [_bdot_nt(cc, bc) * _decay_matrix(ar, incl, n) for cc, bc, ar in zip(cn, bn, acs_r)]
    y_diag = [_bdot(sc, xc) for sc, xc in zip(scores, xdt)]
    s_chunk = [_bdot(xc.T, bc * jnp.exp(tr - ar)) for xc, bc, tr, ar in zip(xdt, bn, tot_r, acs_r)]
    c_dec = [cc * jnp.exp(ar) for cc, ar in zip(cn, acs_r)]
    hs = h_scr[...]
    for ci, sl in enumerate(sls):
        y_bd = y_diag[ci] + _bdot_nt(c_dec[ci], hs) * head
        y_s[sl, :] = _fold4(y_bd, CHUNK)
        hs = hs * jnp.exp(tot_r[ci]) + s_chunk[ci]
    h_scr[...] = hs
    y = (y_s[...] + dsk_ref[...] * xs) * _silu(z_ref[0])
    gw = wg // G_D
    parts = []
    for g in range(G_D):
        yg = y[:, gw * g:gw * (g + 1)]
        ms = jnp.mean(yg * yg, axis=-1, keepdims=True)
        parts.append(yg * lax.rsqrt(ms + EPS))
    y_ref[0] = (jnp.concatenate(parts, axis=1) * ng_ref[...]).astype(BF16)
    hout_ref[0] = h_scr[...]


def _ssd(xbc, z, small, hist, h0, cw, cb, alog, dtb, dsk, ng, tril, masks, sel):
    bsz, l, wc = xbc.shape
    wg = z.shape[2]
    tl = min(256, l)
    kern = functools.partial(_ssd_kernel, tl=tl, wg=wg)
    bt = lambda b_, t_: (b_, t_, 0)
    b0 = lambda b_, t_: (b_, 0, 0)
    return pl.pallas_call(
        kern,
        out_shape=[jax.ShapeDtypeStruct((bsz, l, wg), BF16),
                   jax.ShapeDtypeStruct((bsz, wg, N_D), F32)],
        grid=(bsz, l // tl),
        in_specs=[pl.BlockSpec((1, tl, wc), bt),
                  pl.BlockSpec((1, tl, wg), bt),
                  pl.BlockSpec((1, tl, LANES), bt),
                  pl.BlockSpec((1, SHORT_PAD, wc), b0),
                  pl.BlockSpec((1, wg, N_D), b0),
                  _const_spec((SHORT_PAD, wc)), _const_spec((1, wc)),
                  _const_spec((1, LANES)), _const_spec((1, LANES)), _const_spec((1, wg)), _const_spec((1, wg)),
                  _const_spec((CHUNK, CHUNK)), _const_spec(masks.shape), _const_spec(sel.shape)],
        out_specs=[pl.BlockSpec((1, tl, wg), bt),
                   pl.BlockSpec((1, wg, N_D), b0)],
        scratch_shapes=[pltpu.VMEM((SHORT_PAD + tl, wc), F32),
                        pltpu.VMEM((wg, N_D), F32),
                        pltpu.VMEM((tl, wg), F32), pltpu.VMEM((tl, G_D * N_D), F32), pltpu.VMEM((tl, G_D * N_D), F32),
                        pltpu.VMEM((tl, LANES), F32), pltpu.VMEM((tl, LANES), F32),
                        pltpu.VMEM((tl, wg), F32)],
        compiler_params=_cparams(("parallel", "arbitrary")),
        name="ssd",
    )(xbc, z, small, hist, h0, cw, cb, alog, dtb, dsk, ng, tril, masks, sel)


def _route(logits):
    lane = lax.broadcasted_iota(jnp.int32, logits.shape, 1).astype(F32)
    big = float(LANES)
    lg = jnp.where(lane < E_GROUPS, logits, NEG_BIG)
    mg = jnp.max(lg, axis=-1, keepdims=True)
    sg = jnp.sum(jnp.exp(lg - mg), axis=-1, keepdims=True)
    gidx = jnp.min(jnp.where(lg == mg, lane, big), axis=-1, keepdims=True)
    w_grp = 1.0 / sg
    lo = E_GROUPS + E_PER_GROUP * gidx
    sel = (lane >= lo) & (lane < lo + E_PER_GROUP)
    le = jnp.where(sel, logits, NEG_BIG)
    me = jnp.max(le, axis=-1, keepdims=True)
    pe = jnp.where(sel, jnp.exp(le - me), 0.0)
    p_in = pe / jnp.sum(pe, axis=-1, keepdims=True)
    v1 = jnp.max(p_in, axis=-1, keepdims=True)
    i1 = jnp.min(jnp.where(sel & (p_in == v1), lane, big), axis=-1, keepdims=True)
    rest = sel & (lane != i1)
    p2 = jnp.where(rest, p_in, -1.0)
    v2 = jnp.max(p2, axis=-1, keepdims=True)
    i2 = jnp.min(jnp.where(rest & (p2 == v2), lane, big), axis=-1, keepdims=True)
    den = v1 + v2
    gate = jnp.where(lane == i1, v1 / den, 0.0) + jnp.where(lane == i2, v2 / den, 0.0)
    return gate * w_grp


def _moe_kernel(x_ref, ya_ref, yb_ref, yc_ref, yd_ref, wo_ref, g_ref, wr_ref, br_ref, w13_ref, w2_ref, o_ref, act_s,
                *, f, ne, wg):
    h = x_ref[...]
    for i, y_ref in enumerate((ya_ref, yb_ref, yc_ref, yd_ref)):
        h = h + jnp.dot(y_ref[...], wo_ref[wg * i:wg * (i + 1), :], preferred_element_type=F32)
    ms = jnp.mean(h * h, axis=-1, keepdims=True)
    xn = (h * lax.rsqrt(ms + EPS) * g_ref[...]).astype(BF16)
    gate = _route(jnp.dot(xn, wr_ref[...], preferred_element_type=F32) + br_ref[...])
    for e in range(ne):
        hid = jnp.dot(xn, w13_ref[e], preferred_element_type=F32)
        act = _silu(hid[:, :f]) * hid[:, f:]
        act_s[:, f * e:f * (e + 1)] = (act * gate[:, E_GROUPS + e:E_GROUPS + e + 1]).astype(BF16)
    o_ref[...] = h + jnp.dot(act_s[...], w2_ref[...], preferred_element_type=F32)


def _out_proj_moe(x2d, ys, w_out, g, wr, br, w13, w2s):
    t, d = x2d.shape
    wg = ys[0].shape[1]
    ne, _, f2 = w13.shape
    f = f2 // 2
    tm = min(512, t)
    row = lambda i: (i, 0)
    resident = dict(pipeline_mode=pl.Buffered(1))
    return pl.pallas_call(
        functools.partial(_moe_kernel, f=f, ne=ne, wg=wg),
        out_shape=jax.ShapeDtypeStruct((t, d), F32),
        grid=(t // tm,),
        in_specs=[pl.BlockSpec((tm, d), row)] + [pl.BlockSpec((tm, wg), row)] * 4 + [
                  pl.BlockSpec(w_out.shape, lambda i: (0, 0), **resident),
                  _const_spec((1, d)), _const_spec((d, LANES)), _const_spec((1, LANES)),
                  pl.BlockSpec((ne, d, f2), lambda i: (0, 0, 0), **resident),
                  pl.BlockSpec((ne * f, d), lambda i: (0, 0), **resident)],
        out_specs=pl.BlockSpec((tm, d), row),
        scratch_shapes=[pltpu.VMEM((tm, ne * f), BF16)],
        compiler_params=_cparams(("parallel",)),
        name="out_proj_moe",
    )(x2d, *ys, w_out, g, wr, br, w13, w2s)


def _block_diag_const(n, blk, val):
    r = np.arange(n)
    return jnp.asarray(np.where((r[:, None] // blk) == (r[None, :] // blk), val, 0.0), dtype=BF16)


def _rope_tables(pos, wg, dqk):
    rot = dqk // 4
    half = rot // 2
    inv = jnp.exp(jnp.arange(half, dtype=F32) * (-2.0 / rot) * math.log(ROPE_THETA))
    ang = pos.astype(F32)[:, None] * inv[None, :]
    cos, sin = jnp.cos(ang), jnp.sin(ang)
    l = pos.shape[0]
    ones = jnp.ones((l, dqk - rot), F32)
    zeros = jnp.zeros((l, dqk - rot), F32)
    cos_d = jnp.concatenate([cos, cos, ones], axis=1)
    sin_d = jnp.concatenate([-sin, sin, zeros], axis=1)
    reps = wg // dqk
    return jnp.tile(cos_d, (1, reps)), jnp.tile(sin_d, (1, reps))


def _pad_lanes(v, lane0):
    out = jnp.zeros((1, LANES), F32)
    return out.at[0, lane0:lane0 + v.shape[0]].set(v.astype(F32))


def _layer_params(l, p, wg, d):
    sizes = (2 * wg, 3 * wg, 3 * wg, wg, H_C, H_C, wg, wg + 2 * G_D * N_D, H_D)
    cuts = np.cumsum((0,) + sizes)
    w_in = p['w_in'][l]
    seg = lambda i: w_in[:, cuts[i]:cuts[i + 1]]
    n_small = H_C + H_C + H_D
    w_pad = jnp.concatenate([seg(0), seg(1), seg(2), seg(3), seg(6), seg(7), seg(4), seg(5), seg(8),
                             jnp.zeros((d, LANES - n_small), F32)], axis=1).astype(BF16)
    segs = (sizes[0], sizes[1], sizes[2], sizes[3], sizes[6], sizes[7], LANES)
    dqk = wg // (2 * H_B)
    row = lambda v: v.astype(F32).reshape(1, -1)
    wc = sizes[7]
    lp = dict(
        segs=segs, w_in=w_pad, norm_mix_g=row(p['norm_mix_g'][l]),
        w_out=p['w_out'][l].astype(BF16),
        conf_w=jnp.concatenate([p['conf_conv_w'][l], jnp.zeros((CONV_A_PAD - CONV_A, wg), F32)], axis=0),
        conf_b=row(p['conf_conv_b'][l]), conf_g=row(p['conf_ln_g'][l]), conf_beta=row(p['conf_ln_b'][l]),
        qg=row(jnp.tile(p['diff_qnorm_g'][l], wg // dqk)), kg=row(jnp.tile(p['diff_knorm_g'][l], wg // dqk)),
        lam_p=p['diff_lambda'][l].astype(F32), subln_g=row(p['diff_subln_g'][l]),
        gdn_w=jnp.concatenate([p['gdn_conv_w'][l], jnp.zeros((SHORT_PAD - CONV_SHORT, 3 * wg), F32)], axis=0),
        gdn_alog=_pad_lanes(p['gdn_A_log'][l], H_C), gdn_dtb=_pad_lanes(p['gdn_dt_bias'][l], H_C),
        gdn_og=row(jnp.tile(p['gdn_onorm_g'][l], H_C)),
        ssd_w=jnp.concatenate([p['ssd_conv_w'][l], jnp.zeros((SHORT_PAD - CONV_SHORT, wc), F32)], axis=0),
        ssd_b=row(p['ssd_conv_b'][l]),
        ssd_alog=_pad_lanes(p['ssd_A_log'][l], 2 * H_C), ssd_dtb=_pad_lanes(p['ssd_dt_bias'][l], 2 * H_C),
        ssd_dsk=row(jnp.repeat(p['ssd_D'][l], wg // H_D)), ssd_ng=row(p['ssd_norm_g'][l]),
        norm_ffn_g=row(p['norm_ffn_g'][l]),
        moe_wr=jnp.concatenate([p['moe_w_group'][l], p['moe_w_expert'][l],
                                jnp.zeros((d, LANES - E_GROUPS - N_EXPERTS), F32)], axis=1).astype(BF16),
        moe_br=jnp.concatenate([p['moe_b_group'][l], p['moe_b_expert'][l],
                                jnp.zeros((LANES - E_GROUPS - N_EXPERTS,), F32)]).reshape(1, LANES),
        moe_w13=jnp.concatenate([p['moe_w1'][l], p['moe_w3'][l]], axis=-1).astype(BF16),
        moe_w2=p['moe_w2'][l].astype(BF16).reshape(-1, d),
    )
    return lp


def _trunk_layer(x, lp, consts, pos, lam_init, buf_a, k_past, v_past, buf_c, s_c, buf_d, h_d):
    bsz, l, d = x.shape
    wg = d // N_MIXERS
    dqk = wg // (2 * H_B)
    dv = wg // H_B
    assert wg // H_C == CHUNK and wg // H_D == CHUNK and H_C == H_D == N_MIXERS
    assert l % CHUNK == 0 and l >= CONV_A_PAD
    x2d = x.reshape(bsz * l, d)
    u_a, u_b, u_c, u_cz, u_dz, u_d, u_s = _in_proj(x2d, lp['norm_mix_g'], lp['w_in'], lp['segs'])
    r3 = lambda a: a.reshape(bsz, l, a.shape[-1])
    u_a, u_b, u_c, u_cz, u_dz, u_d, u_s = map(r3, (u_a, u_b, u_c, u_cz, u_dz, u_d, u_s))

    cache_pad = jnp.concatenate([jnp.zeros((bsz, CONV_A_PAD - (CONV_A - 1), wg), F32), buf_a.astype(F32)], axis=1)
    y_a, tail_a = _conformer(u_a, cache_pad, lp['conf_w'], lp['conf_b'], lp['conf_g'], lp['conf_beta'],
                             consts['mean_a'])
    new_buf_a = tail_a[:, CONV_A_PAD - (CONV_A - 1):, :]

    cos_t, sin_t = _rope_tables(pos, wg, dqk)
    q_b, kt_b, v_b, k_new, v_new = _attn_prep(u_b, cos_t, sin_t, lp['qg'], lp['kg'], consts['mean_qk'],
                                              consts['eye'], dqk)
    if k_past is not None:
        p = k_past.shape[1]
        ktp = _transpose_keys(k_past.reshape(bsz, p, wg).astype(F32), consts['eye'])
        vp4 = v_past.reshape(bsz, p, H_B, dv).astype(BF16)
        one = jnp.zeros((bsz, p, H_B, dv), BF16).at[..., 0].set(1.0)
        vp = jnp.concatenate([vp4, one], axis=-1).reshape(bsz, p, 2 * wg)
    else:
        ktp, vp = None, None
    y_b = _flash(q_b, kt_b, v_b, ktp, vp, lp['lam_p'], lp['subln_g'], lam_init, dqk, dv)

    hist_c = jnp.concatenate([jnp.zeros((bsz, SHORT_PAD - (CONV_SHORT - 1), 3 * wg), F32), buf_c.astype(F32)], axis=1)
    s0_bd = jnp.tile(s_c.astype(F32).reshape(bsz, wg, wg // H_C), (1, 1, H_C)) * consts['head_mask']
    y_c, s_bd = _gdn(u_c, u_cz, u_s, hist_c, s0_bd, lp['gdn_w'], lp['gdn_alog'], lp['gdn_dtb'], lp['gdn_og'],
                     consts['ones_c'], consts['tril'], consts['bd_masks'], consts['sel'])
    dk = wg // H_C
    s_new = jnp.stack([s_bd[:, dk * h:dk * (h + 1), dk * h:dk * (h + 1)] for h in range(H_C)], axis=1)
    new_buf_c = u_c[:, l - (CONV_SHORT - 1):, :]

    wc = u_d.shape[-1]
    hist_d = jnp.concatenate([jnp.zeros((bsz, SHORT_PAD - (CONV_SHORT - 1), wc), F32), buf_d.astype(F32)], axis=1)
    h0 = h_d.astype(F32).reshape(bsz, wg, N_D)
    y_d, h_new = _ssd(u_d, u_dz, u_s, hist_d, h0, lp['ssd_w'], lp['ssd_b'], lp['ssd_alog'], lp['ssd_dtb'],
                      lp['ssd_dsk'], lp['ssd_ng'], consts['tril'], consts['bd_masks'], consts['sel'])
    h_new = h_new.reshape(bsz, H_D, wg // H_D, N_D)
    new_buf_d = u_d[:, l - (CONV_SHORT - 1):, :]

    r2 = lambda a: a.reshape(bsz * l, wg)
    y2d = _out_proj_moe(x2d, [r2(y_a), r2(y_b), r2(y_c), r2(y_d)], lp['w_out'], lp['norm_ffn_g'], lp['moe_wr'],
                        lp['moe_br'], lp['moe_w13'], lp['moe_w2'])
    y = y2d.reshape(bsz, l, d)
    k_new = k_new.reshape(bsz, l, H_B, 2, dqk)
    v_new = v_new.reshape(bsz, l, H_B, dv)
    return y, new_buf_a, k_new, v_new, new_buf_c, s_new, new_buf_d, h_new


@jax.jit
def _forward(x_prompt, x_sample, cache_conv_conformer, cache_k_diff, cache_v_diff, cache_conv_delta,
             state_delta, cache_conv_ssd, state_ssd, params):
    bp, lp_, d = x_prompt.shape
    ls = x_sample.shape[1]
    depth = params['w_in'].shape[0]
    past_len = cache_k_diff.shape[2]
    wg = d // N_MIXERS
    dqk = wg // (2 * H_B)
    consts = dict(
        mean_a=_block_diag_const(wg, wg // A_GROUPS, 1.0 / (wg // A_GROUPS)),
        mean_qk=_block_diag_const(wg, dqk, 1.0 / dqk),
        ones_c=_block_diag_const(wg, wg // H_C, 1.0),
        eye=jnp.eye(wg, dtype=BF16),
        tril=jnp.asarray(np.tril(np.ones((CHUNK, CHUNK))), dtype=BF16),
        head_mask=jnp.asarray(np.kron(np.eye(H_C), np.ones((wg // H_C, wg // H_C))), dtype=F32),
        bd_masks=_bd_mask_consts(wg),
        sel=_sel_consts(wg, (0, H_C, 2 * H_C)),
    )
    pos_p = jnp.arange(lp_, dtype=jnp.int32)
    pos_s = past_len + jnp.arange(ls, dtype=jnp.int32)
    y_p, y_s = x_prompt, x_sample
    new_p = [[] for _ in range(7)]
    new_s = [[] for _ in range(7)]
    wc = wg + 2 * G_D * N_D
    for l in range(depth):
        lam_init = 0.8 - 0.6 * math.exp(-0.3 * l)
        lp = _layer_params(l, params, wg, d)
        y_p, *st_p = _trunk_layer(
            y_p, lp, consts, pos_p, lam_init,
            jnp.zeros((bp, CONV_A - 1, wg), F32), None, None,
            jnp.zeros((bp, CONV_SHORT - 1, 3 * wg), F32), jnp.zeros((bp, H_C, wg // H_C, wg // H_C), F32),
            jnp.zeros((bp, CONV_SHORT - 1, wc), F32), jnp.zeros((bp, H_D, wg // H_D, N_D), F32))
        y_s, *st_s = _trunk_layer(
            y_s, lp, consts, pos_s, lam_init,
            cache_conv_conformer[l], cache_k_diff[l], cache_v_diff[l],
            cache_conv_delta[l], state_delta[l], cache_conv_ssd[l], state_ssd[l])
        for i in range(7):
            new_p[i].append(st_p[i])
            new_s[i].append(st_s[i])
    outs_p = [jnp.stack(s, axis=0) for s in new_p]
    outs_s = [jnp.stack(s, axis=0) for s in new_s]
    return (y_p, y_s, *outs_p, *outs_s)


def kernel(x_prompt, x_sample, cache_conv_conformer, cache_k_diff, cache_v_diff, cache_conv_delta, state_delta, cache_conv_ssd, state_ssd, norm_mix_g, w_in, w_out, conf_conv_w, conf_conv_b, conf_ln_g, conf_ln_b, diff_qnorm_g, diff_knorm_g, diff_lambda, diff_subln_g, gdn_conv_w, gdn_A_log, gdn_dt_bias, gdn_onorm_g, ssd_conv_w, ssd_conv_b, ssd_A_log, ssd_dt_bias, ssd_D, ssd_norm_g, norm_ffn_g, moe_w_group, moe_b_group, moe_w_expert, moe_b_expert, moe_w1, moe_w3, moe_w2):
    params = dict(
        norm_mix_g=norm_mix_g, w_in=w_in, w_out=w_out, conf_conv_w=conf_conv_w, conf_conv_b=conf_conv_b,
        conf_ln_g=conf_ln_g, conf_ln_b=conf_ln_b, diff_qnorm_g=diff_qnorm_g, diff_knorm_g=diff_knorm_g,
        diff_lambda=diff_lambda, diff_subln_g=diff_subln_g, gdn_conv_w=gdn_conv_w, gdn_A_log=gdn_A_log,
        gdn_dt_bias=gdn_dt_bias, gdn_onorm_g=gdn_onorm_g, ssd_conv_w=ssd_conv_w, ssd_conv_b=ssd_conv_b,
        ssd_A_log=ssd_A_log, ssd_dt_bias=ssd_dt_bias, ssd_D=ssd_D, ssd_norm_g=ssd_norm_g,
        norm_ffn_g=norm_ffn_g, moe_w_group=moe_w_group, moe_b_group=moe_b_group, moe_w_expert=moe_w_expert,
        moe_b_expert=moe_b_expert, moe_w1=moe_w1, moe_w3=moe_w3, moe_w2=moe_w2)
    return _forward(x_prompt, x_sample, cache_conv_conformer, cache_k_diff, cache_v_diff, cache_conv_delta,
                    state_delta, cache_conv_ssd, state_ssd, params)
```
